```python
import math
import jax, jax.numpy as jnp
from jax import lax
import numpy as np

D_MODEL = 1024
BATCH = 4
SEQ = 4096
DEPTH = 4

N_MIXERS = 2
N_SSM_LAYERS = (DEPTH + 1) // 2
N_ATTN_LAYERS = DEPTH // 2
SSM_EXPAND = 2
D_INNER = SSM_EXPAND * D_MODEL
SSM_HEAD_DIM = 64
SSM_HEADS = D_INNER // SSM_HEAD_DIM
SSM_GROUPS = 8
SSM_STATE = 128
SSM_CONV = 4
SSM_CHUNK = 128
SSM_CONV_DIM = D_INNER + 2 * SSM_GROUPS * SSM_STATE
SSM_IN_DIM = 2 * D_INNER + 2 * SSM_GROUPS * SSM_STATE + SSM_HEADS
ATTN_HEAD_DIM = 64
ATTN_HEADS = D_MODEL // (2 * ATTN_HEAD_DIM)
ATTN_V_DIM = 2 * ATTN_HEAD_DIM
Q_BLOCK = 128
D_FF = 2816
FFN_CONV = 3
RMS_EPS = 1e-6

kernel_name = "hybrid_ssd_diffattn_convffn"


def rmsnorm(x, g):
    xf = x.astype(jnp.float32)
    y = xf * lax.rsqrt(jnp.mean(xf * xf, axis=-1, keepdims=True) + RMS_EPS)
    return (y * g.astype(jnp.float32)).astype(x.dtype)


def causal_dwconv(x, w, b):
    k_w = w.shape[0]
    s = x.shape[1]
    xp = jnp.pad(x, ((0, 0), (k_w - 1, 0), (0, 0)))
    y = xp[:, 0:s] * w[0]
    for k in range(1, k_w):
        y = y + xp[:, k:k + s] * w[k]
    return y + b


def ssd_chunked(xh, dA, Bm, Cm):
    b, s, h, p = xh.shape
    g, n = Bm.shape[2], Bm.shape[3]
    r = h // g
    l = SSM_CHUNK
    c = s // l
    x = xh.reshape(b, c, l, g, r, p)
    A = dA.reshape(b, c, l, g, r).transpose(0, 1, 3, 4, 2)
    Bc = Bm.reshape(b, c, l, g, n)
    Cc = Cm.reshape(b, c, l, g, n)
    A_cs = jnp.cumsum(A, axis=-1)
    causal = jnp.tril(jnp.ones((l, l), dtype=bool))
    seg = A_cs[..., :, None] - A_cs[..., None, :]
    L = jnp.exp(jnp.where(causal, seg, -jnp.inf))
    CB = jnp.einsum('bclgn,bcsgn->bcgls', Cc, Bc)
    y_diag = jnp.einsum('bcgls,bcgrls,bcsgrp->bclgrp', CB, L, x)
    decay = jnp.exp(A_cs[..., -1:] - A_cs)
    states = jnp.einsum('bcsgn,bcgrs,bcsgrp->bcgrpn', Bc, decay, x)
    chunk_decay = jnp.exp(A_cs[..., -1])

    def step(carry, inp):
        st, dec = inp
        return carry * dec[..., None, None] + st, carry

    init = jnp.zeros_like(states[:, 0])
    _, prev = lax.scan(step, init, (states.transpose(1, 0, 2, 3, 4, 5),
                                    chunk_decay.transpose(1, 0, 2, 3)))
    prev = prev.transpose(1, 0, 2, 3, 4, 5)
    y_off = jnp.einsum('bclgn,bcgrpn,bcgrl->bclgrp', Cc, prev, jnp.exp(A_cs))
    return (y_diag + y_off).reshape(b, s, h, p)


def mamba2_mixer(u, w_in, conv_w, conv_b, dt_bias, A_log, D_skip, norm_g, w_out):
    b, s, _ = u.shape
    f32 = jnp.float32
    proj = jnp.einsum('bsd,de->bse', u, w_in)
    z, xBC, dt = jnp.split(proj, [D_INNER, D_INNER + SSM_CONV_DIM], axis=-1)
    xBC = jax.nn.silu(causal_dwconv(xBC, conv_w, conv_b))
    xs, Bm, Cm = jnp.split(xBC, [D_INNER, D_INNER + SSM_GROUPS * SSM_STATE], axis=-1)
    xs = xs.reshape(b, s, SSM_HEADS, SSM_HEAD_DIM).astype(f32)
    Bm = Bm.reshape(b, s, SSM_GROUPS, SSM_STATE).astype(f32)
    Cm = Cm.reshape(b, s, SSM_GROUPS, SSM_STATE).astype(f32)
    dt = jax.nn.softplus(dt.astype(f32) + dt_bias.astype(f32))
    A = -jnp.exp(A_log.astype(f32))
    y = ssd_chunked(xs * dt[..., None], dt * A, Bm, Cm)
    y = y + xs * D_skip.astype(f32)[:, None]
    y = y.reshape(b, s, D_INNER) * jax.nn.silu(z.astype(f32))
    yg = y.reshape(b, s, SSM_GROUPS, D_INNER // SSM_GROUPS)
    yg = yg * lax.rsqrt(jnp.mean(yg * yg, axis=-1, keepdims=True) + RMS_EPS)
    y = yg.reshape(b, s, D_INNER) * norm_g.astype(f32)
    return jnp.einsum('bse,ed->bsd', y.astype(u.dtype), w_out)


def alibi_slopes(h):
    return jnp.exp2(-8.0 * jnp.arange(1, h + 1, dtype=jnp.float32) / h)


def diff_attention(u, w_qkv, lq1, lk1, lq2, lk2, subln_g, w_out, lambda_init):
    b, s, _ = u.shape
    f32 = jnp.float32
    H, dh = ATTN_HEADS, ATTN_HEAD_DIM
    qkv = jnp.einsum('bsd,de->bse', u, w_qkv)
    q, k, v = jnp.split(qkv, 3, axis=-1)
    q = q.reshape(b, s, H, 2, dh)
    k = k.reshape(b, s, H, 2, dh)
    v = v.reshape(b, s, H, ATTN_V_DIM)
    lam = (jnp.exp(jnp.sum(lq1.astype(f32) * lk1.astype(f32)))
           - jnp.exp(jnp.sum(lq2.astype(f32) * lk2.astype(f32))) + lambda_init)
    scale = dh ** -0.5
    slopes = alibi_slopes(H)
    kpos = jnp.arange(s)
    n_blk = s // Q_BLOCK

    def block(i):
        start = i * Q_BLOCK
        qb = lax.dynamic_slice_in_dim(q, start, Q_BLOCK, axis=1)
        sc = jnp.einsum('bqhtd,bkhtd->bhtqk', qb, k).astype(f32) * scale
        qpos = start + jnp.arange(Q_BLOCK)
        dist = (qpos[:, None] - kpos[None, :]).astype(f32)
        bias = -slopes[:, None, None, None] * dist
        sc = jnp.where(dist >= 0, sc + bias, -jnp.inf)
        pr = jax.nn.softmax(sc, axis=-1)
        a = pr[:, :, 0] - lam * pr[:, :, 1]
        return jnp.einsum('bhqk,bkhe->bqhe', a.astype(v.dtype), v)

    out = lax.map(block, jnp.arange(n_blk))
    out = out.transpose(1, 0, 2, 3, 4).reshape(b, s, H, ATTN_V_DIM)
    out = rmsnorm(out, subln_g) * (1.0 - lambda_init)
    return jnp.einsum('bse,ed->bsd', out.reshape(b, s, H * ATTN_V_DIM), w_out)


def conv_ffn(u, w_up, conv_w, conv_b, w_down):
    h = jnp.einsum('bsd,df->bsf', u, w_up)
    h = causal_dwconv(h, conv_w, conv_b)
    gate, up = jnp.split(h, 2, axis=-1)
    return jnp.einsum('bsf,fd->bsd', jax.nn.silu(gate) * up, w_down)


def setup_inputs(seed: int = 0) -> dict:
    key = jax.random.key(seed)
    ks = jax.random.split(key, 24)
    f32 = jnp.float32

    def nrm(k, shape, scale):
        return jax.random.normal(k, shape, f32) * scale

    NS, NA = N_SSM_LAYERS, N_ATTN_LAYERS
    dt0 = jnp.exp(jax.random.uniform(ks[4], (NS, SSM_HEADS), f32, math.log(1e-3), math.log(1e-1)))
    return {
        "x": nrm(ks[0], (BATCH, SEQ, D_MODEL), 1.0),
        "ssm_w_in": nrm(ks[1], (NS, D_MODEL, SSM_IN_DIM), D_MODEL ** -0.5),
        "ssm_conv_w": nrm(ks[2], (NS, SSM_CONV, SSM_CONV_DIM), SSM_CONV ** -0.5),
        "ssm_conv_b": nrm(ks[3], (NS, SSM_CONV_DIM), 0.02),
        "ssm_dt_bias": dt0 + jnp.log(-jnp.expm1(-dt0)),
        "ssm_A_log": jnp.log(jax.random.uniform(ks[5], (NS, SSM_HEADS), f32, 1.0, 16.0)),
        "ssm_D": 1.0 + nrm(ks[6], (NS, SSM_HEADS), 0.1),
        "ssm_norm_g": 1.0 + nrm(ks[7], (NS, D_INNER), 0.1),
        "ssm_w_out": nrm(ks[8], (NS, D_INNER, D_MODEL), D_INNER ** -0.5),
        "attn_w_qkv": nrm(ks[9], (NA, D_MODEL, 3 * D_MODEL), D_MODEL ** -0.5),
        "attn_lambda_q1": nrm(ks[10], (NA, ATTN_HEAD_DIM), 0.1),
        "attn_lambda_k1": nrm(ks[11], (NA, ATTN_HEAD_DIM), 0.1),
        "attn_lambda_q2": nrm(ks[12], (NA, ATTN_HEAD_DIM), 0.1),
        "attn_lambda_k2": nrm(ks[13], (NA, ATTN_HEAD_DIM), 0.1),
        "attn_subln_g": 1.0 + nrm(ks[14], (NA, ATTN_V_DIM), 0.1),
        "attn_w_out": nrm(ks[15], (NA, D_MODEL, D_MODEL), D_MODEL ** -0.5),
        "norm_mix_pre": 1.0 + nrm(ks[16], (DEPTH, D_MODEL), 0.1),
        "norm_mix_post": 1.0 + nrm(ks[17], (DEPTH, D_MODEL), 0.1),
        "norm_ffn_pre": 1.0 + nrm(ks[18], (DEPTH, D_MODEL), 0.1),
        "norm_ffn_post": 1.0 + nrm(ks[19], (DEPTH, D_MODEL), 0.1),
        "ffn_w_up": nrm(ks[20], (DEPTH, D_MODEL, 2 * D_FF), D_MODEL ** -0.5),
        "ffn_conv_w": nrm(ks[21], (DEPTH, FFN_CONV, 2 * D_FF), FFN_CONV ** -0.5),
        "ffn_conv_b": nrm(ks[22], (DEPTH, 2 * D_FF), 0.02),
        "ffn_w_down": nrm(ks[23], (DEPTH, D_FF, D_MODEL), D_FF ** -0.5),
    }


def reference(x, ssm_w_in, ssm_conv_w, ssm_conv_b, ssm_dt_bias, ssm_A_log, ssm_D, ssm_norm_g,
              ssm_w_out, attn_w_qkv, attn_lambda_q1, attn_lambda_k1, attn_lambda_q2,
              attn_lambda_k2, attn_subln_g, attn_w_out, norm_mix_pre, norm_mix_post,
              norm_ffn_pre, norm_ffn_post, ffn_w_up, ffn_conv_w, ffn_conv_b, ffn_w_down):
    for i in range(DEPTH):
        j = i // N_MIXERS
        hn = rmsnorm(x, norm_mix_pre[i])
        if i % N_MIXERS == 0:
            m = mamba2_mixer(hn, ssm_w_in[j], ssm_conv_w[j], ssm_conv_b[j], ssm_dt_bias[j],
                             ssm_A_log[j], ssm_D[j], ssm_norm_g[j], ssm_w_out[j])
        else:
            lambda_init = 0.8 - 0.6 * math.exp(-0.3 * i)
            m = diff_attention(hn, attn_w_qkv[j], attn_lambda_q1[j], attn_lambda_k1[j],
                               attn_lambda_q2[j], attn_lambda_k2[j], attn_subln_g[j],
                               attn_w_out[j], lambda_init)
        x = x + rmsnorm(m, norm_mix_post[i])
        hn = rmsnorm(x, norm_ffn_pre[i])
        f = conv_ffn(hn, ffn_w_up[i], ffn_conv_w[i], ffn_conv_b[i], ffn_w_down[i])
        x = x + rmsnorm(f, norm_ffn_post[i])
    return x
```

```python
import functools
import math

import jax
import jax.numpy as jnp
from jax import lax
from jax.experimental import pallas as pl
from jax.experimental.pallas import tpu as pltpu

F32 = jnp.float32
BF16 = jnp.bfloat16

D_MODEL = 1024
D_INNER = 2048
SSM_HEAD_DIM = 64
SSM_HEADS = 32
SSM_GROUPS = 8
SSM_STATE = 128
SSM_CONV = 4
SSM_CHUNK = 128
SSM_BC_DIM = SSM_GROUPS * SSM_STATE
SSM_CONV_DIM = D_INNER + 2 * SSM_BC_DIM
HEADS_PER_GROUP = SSM_HEADS // SSM_GROUPS
GROUP_WIDTH = D_INNER // SSM_GROUPS
ATTN_HEAD_DIM = 64
ATTN_HEADS = 8
ATTN_V_DIM = 128
D_FF = 2816
FFN_CONV = 3
RMS_EPS = 1e-6

LANES = 128
SUBLANES = 8
V7X_VMEM_LIMIT_BYTES = 56 * 1024 * 1024

ROW_TILE = 512
ATTN_TILE = 512
N_CHUNK = 512


def _params(n_axes):
    return pltpu.CompilerParams(
        dimension_semantics=("arbitrary",) * n_axes,
        vmem_limit_bytes=V7X_VMEM_LIMIT_BYTES,
    )


def _resident(shape):
    zeros = (0,) * len(shape)
    return pl.BlockSpec(shape, lambda *_: zeros, pipeline_mode=pl.Buffered(1))


def _rms(x, g):
    return x * lax.rsqrt(jnp.mean(x * x, axis=-1, keepdims=True) + RMS_EPS) * g


def _silu(x):
    return x / (1.0 + jnp.exp(-x))


def _dot(a, b):
    return jnp.dot(a, b, preferred_element_type=F32)


def _dot_nt(a, b):
    return lax.dot_general(a, b, (((1,), (1,)), ((), ())), preferred_element_type=F32)


def _shift_rows(h, prev, k):
    r = pltpu.roll(h, k, 0)
    pr = pltpu.roll(prev, k, 0)
    row = lax.broadcasted_iota(jnp.int32, pr.shape, 0)
    top = jnp.where(row < k, pr, r[0:SUBLANES])
    return jnp.concatenate([top, r[SUBLANES:]], axis=0)


def _causal_conv(h, prev, w_ref, b_ref, cols, width):
    y = h * w_ref[width - 1:width, cols] + b_ref[:, cols]
    for k in range(1, width):
        y = y + _shift_rows(h, prev, k) * w_ref[width - 1 - k:width - k, cols]
    return y


def _ssm_inproj_kernel(tiles_per_seq, x_ref, g_ref, wz_ref, wx_ref, wdt_ref, cw_ref, cb_ref,
                       z_ref, xbc_ref, dt_ref, carry_ref):
    tm = x_ref.shape[0]

    @pl.when(pl.program_id(0) % tiles_per_seq == 0)
    def _():
        carry_ref[...] = jnp.zeros_like(carry_ref)

    hn = _rms(x_ref[...], g_ref[...]).astype(BF16)
    dt_ref[...] = _dot(hn, wdt_ref[...])
    for c in range(D_INNER // N_CHUNK):
        cols = slice(c * N_CHUNK, (c + 1) * N_CHUNK)
        z_ref[:, cols] = _dot(hn, wz_ref[:, cols]).astype(BF16)
    for c in range(SSM_CONV_DIM // N_CHUNK):
        cols = slice(c * N_CHUNK, (c + 1) * N_CHUNK)
        h = _dot(hn, wx_ref[:, cols])
        prev = carry_ref[:, cols]
        carry_ref[:, cols] = h[tm - SUBLANES:tm, :]
        y = _causal_conv(h, prev, cw_ref, cb_ref, cols, SSM_CONV)
        xbc_ref[:, cols] = _silu(y).astype(BF16)


def _ssm_inproj(x, g, wz, wx, wdt, conv_w, conv_b, seq):
    t = x.shape[0]
    tm = min(ROW_TILE, seq)
    row = lambda i: (i, 0)
    return pl.pallas_call(
        functools.partial(_ssm_inproj_kernel, seq // tm),
        grid=(t // tm,),
        in_specs=[
            pl.BlockSpec((tm, D_MODEL), row),
            _resident((1, D_MODEL)),
            _resident((D_MODEL, D_INNER)),
            _resident((D_MODEL, SSM_CONV_DIM)),
            _resident((D_MODEL, LANES)),
            _resident((SSM_CONV, SSM_CONV_DIM)),
            _resident((1, SSM_CONV_DIM)),
        ],
        out_specs=[
            pl.BlockSpec((tm, D_INNER), row),
            pl.BlockSpec((tm, SSM_CONV_DIM), row),
            pl.BlockSpec((tm, LANES), row),
        ],
        out_shape=[
            jax.ShapeDtypeStruct((t, D_INNER), BF16),
            jax.ShapeDtypeStruct((t, SSM_CONV_DIM), BF16),
            jax.ShapeDtypeStruct((t, LANES), F32),
        ],
        scratch_shapes=[pltpu.VMEM((SUBLANES, SSM_CONV_DIM), F32)],
        compiler_params=_params(1),
        name="ssm_inproj",
    )(x, g, wz, wx, wdt, conv_w, conv_b)


def _ssd_kernel(xbc_ref, z_ref, dt_ref, dtb_ref, alog_ref, dexp_ref, ng_ref, e_ref,
                y_ref, state_ref):
    L = SSM_CHUNK

    @pl.when(pl.program_id(1) == 0)
    def _():
        state_ref[...] = jnp.zeros_like(state_ref)

    dt_raw = dt_ref[...] + dtb_ref[...]
    dt = jnp.maximum(dt_raw, 0.0) + jnp.log1p(jnp.exp(-jnp.abs(dt_raw)))
    d_a = dt * (-jnp.exp(alog_ref[...]))
    row = lax.broadcasted_iota(jnp.int32, (L, LANES), 0)
    lane = lax.broadcasted_iota(jnp.int32, (L, LANES), 1)
    acs = d_a
    k = 1
    while k < L:
        acs = acs + jnp.where(row >= k, pltpu.roll(acs, k, 0), 0.0)
        k *= 2
    acs_t = acs.T
    dt_t = dt.T
    w_t = dt_t * jnp.exp(acs_t[:, L - 1:L] - acs_t)
    exp_acs = jnp.exp(acs)

    cd = jnp.broadcast_to(exp_acs[L - 1:L, :], (SUBLANES, LANES))
    cd_hi = cd.astype(BF16)
    cd_lo = (cd - cd_hi.astype(F32)).astype(BF16)
    cd_wide = (_dot(cd_hi, e_ref[...]) + _dot(cd_lo, e_ref[...]))[0:1, :]

    causal = row >= lane
    left = lane < SSM_HEAD_DIM

    for g in range(SSM_GROUPS):
        b_g = xbc_ref[:, D_INNER + g * SSM_STATE:D_INNER + (g + 1) * SSM_STATE]
        c_g = xbc_ref[:, D_INNER + SSM_BC_DIM + g * SSM_STATE:
                      D_INNER + SSM_BC_DIM + (g + 1) * SSM_STATE]
        cb = _dot_nt(c_g, b_g)
        b_t = b_g.astype(F32).T
        c_f = c_g.astype(F32)
        ys = []
        for pr in range(HEADS_PER_GROUP // 2):
            heads = (g * HEADS_PER_GROUP + 2 * pr, g * HEADS_PER_GROUP + 2 * pr + 1)
            cols = slice((g * 2 + pr) * LANES, (g * 2 + pr + 1) * LANES)
            scols = slice(pr * LANES, (pr + 1) * LANES)
            xp = xbc_ref[:, cols].astype(F32)
            sp = state_ref[g, :, scols]
            m_parts, c_parts, bw_parts = [], [], []
            for h in heads:
                a_col = jnp.broadcast_to(acs[:, h:h + 1], (L, L))
                a_row = jnp.broadcast_to(acs_t[h:h + 1, :], (L, L))
                decay = jnp.exp(jnp.where(causal, a_col - a_row, -jnp.inf))
                m_parts.append((cb * decay * jnp.broadcast_to(dt_t[h:h + 1, :], (L, L))).astype(BF16))
                c_parts.append((c_f * jnp.broadcast_to(exp_acs[:, h:h + 1], (L, L))).astype(BF16))
                bw_parts.append((b_t * jnp.broadcast_to(w_t[h:h + 1, :], (L, L))).astype(BF16))
            x_l = jnp.where(left, xp, 0.0).astype(BF16)
            x_r = jnp.where(left, 0.0, xp).astype(BF16)
            s_l = jnp.where(left, sp, 0.0).astype(BF16)
            s_r = jnp.where(left, 0.0, sp).astype(BF16)
            x_blk = jnp.concatenate([x_l, x_r], axis=0)
            y_p = _dot(jnp.concatenate(m_parts + c_parts, axis=1),
                       jnp.concatenate([x_blk, s_l, s_r], axis=0))
            y_p = y_p + xp * dexp_ref[:, cols]
            state_ref[g, :, scols] = sp * cd_wide[:, cols] + _dot(
                jnp.concatenate(bw_parts, axis=1), x_blk)
            ys.append(y_p)
        gcols = slice(g * GROUP_WIDTH, (g + 1) * GROUP_WIDTH)
        y_g = jnp.concatenate(ys, axis=1) * _silu(z_ref[:, gcols].astype(F32))
        y_ref[:, gcols] = _rms(y_g, ng_ref[:, gcols]).astype(BF16)


def _ssd(xbc, z, dt, dt_bias, a_log, d_wide, norm_g, expand, batch, seq):
    t = xbc.shape[0]
    n_chunks = seq // SSM_CHUNK
    row = lambda b, c: (b * n_chunks + c, 0)
    return pl.pallas_call(
        _ssd_kernel,
        grid=(batch, n_chunks),
        in_specs=[
            pl.BlockSpec((SSM_CHUNK, SSM_CONV_DIM), row),
            pl.BlockSpec((SSM_CHUNK, D_INNER), row),
            pl.BlockSpec((SSM_CHUNK, LANES), row),
            _resident((1, LANES)),
            _resident((1, LANES)),
            _resident((1, D_INNER)),
            _resident((1, D_INNER)),
            _resident((LANES, D_INNER)),
        ],
        out_specs=pl.BlockSpec((SSM_CHUNK, D_INNER), row),
        out_shape=jax.ShapeDtypeStruct((t, D_INNER), BF16),
        scratch_shapes=[pltpu.VMEM((SSM_GROUPS, SSM_STATE, GROUP_WIDTH), F32)],
        compiler_params=_params(2),
        name="ssd_scan",
    )(xbc, z, dt, dt_bias, a_log, d_wide, norm_g, expand)


def _out_proj_kernel(y_ref, w_ref, g_ref, x_ref, o_ref):
    m = _dot(y_ref[...], w_ref[...])
    o_ref[...] = x_ref[...] + _rms(m, g_ref[...])


def _out_proj(y, w, g, x, seq):
    t, k = y.shape
    tm = min(ROW_TILE, seq)
    row = lambda i: (i, 0)
    return pl.pallas_call(
        _out_proj_kernel,
        grid=(t // tm,),
        in_specs=[
            pl.BlockSpec((tm, k), row),
            _resident((k, D_MODEL)),
            _resident((1, D_MODEL)),
            pl.BlockSpec((tm, D_MODEL), row),
        ],
        out_specs=pl.BlockSpec((tm, D_MODEL), row),
        out_shape=jax.ShapeDtypeStruct((t, D_MODEL), F32),
        compiler_params=_params(1),
        name="mixer_out_proj",
    )(y, w, g, x)


def _qkv_kernel(x_ref, g_ref, wqt_ref, wk_ref, wvt_ref, qt_ref, k_ref, vt_ref):
    hn = _rms(x_ref[...], g_ref[...]).astype(BF16)
    scale = ATTN_HEAD_DIM ** -0.5
    for c in range(D_MODEL // N_CHUNK):
        rows = slice(c * N_CHUNK, (c + 1) * N_CHUNK)
        qt_ref[0, rows, :] = (_dot_nt(wqt_ref[rows, :], hn) * scale).astype(BF16)
        vt_ref[0, rows, :] = _dot_nt(wvt_ref[rows, :], hn).astype(BF16)
        k_ref[:, rows] = _dot(hn, wk_ref[:, rows]).astype(BF16)


def _qkv(x, g, wqt, wk, wvt, seq):
    t = x.shape[0]
    tm = min(ATTN_TILE, seq)
    row = lambda i: (i, 0)
    tile = lambda i: (i, 0, 0)
    return pl.pallas_call(
        _qkv_kernel,
        grid=(t // tm,),
        in_specs=[
            pl.BlockSpec((tm, D_MODEL), row),
            _resident((1, D_MODEL)),
            _resident((D_MODEL, D_MODEL)),
            _resident((D_MODEL, D_MODEL)),
            _resident((D_MODEL, D_MODEL)),
        ],
        out_specs=[
            pl.BlockSpec((1, D_MODEL, tm), tile),
            pl.BlockSpec((tm, D_MODEL), row),
            pl.BlockSpec((1, D_MODEL, tm), tile),
        ],
        out_shape=[
            jax.ShapeDtypeStruct((t // tm, D_MODEL, tm), BF16),
            jax.ShapeDtypeStruct((t, D_MODEL), BF16),
            jax.ShapeDtypeStruct((t // tm, D_MODEL, tm), BF16),
        ],
        compiler_params=_params(1),
        name="attn_qkv",
    )(x, g, wqt, wk, wvt)


def _attn_kernel(lambda_init, qt_ref, k_ref, vt_ref, pos_ref, slope_ref, lam_ref, g_ref,
                 o_ref, m_ref, l_ref, acc_ref):
    tq = qt_ref.shape[2]
    tk = tq
    qi = pl.program_id(2)

    slope = jnp.concatenate([slope_ref[0]] * (tq // LANES), axis=1)
    s_hi = slope.astype(BF16).astype(F32)
    s_lo = slope - s_hi
    r8 = lax.broadcasted_iota(jnp.int32, (SUBLANES, tq), 0)
    feat = jnp.where(r8 == 0, 16.0 * s_hi,
                     jnp.where(r8 == 1, s_hi,
                               jnp.where(r8 == 2, 16.0 * s_lo,
                                         jnp.where(r8 == 3, s_lo, 0.0)))).astype(BF16)
    aug = jnp.concatenate([feat, jnp.zeros((LANES - SUBLANES, tq), BF16)], axis=0)
    zero_half = jnp.zeros((ATTN_HEAD_DIM, tq), BF16)
    q_all = qt_ref[0]
    q_aug = (jnp.concatenate([q_all[0:ATTN_HEAD_DIM], zero_half, aug], axis=0),
             jnp.concatenate([zero_half, q_all[ATTN_HEAD_DIM:], aug], axis=0))
    slope_row = slope[0:1, :]
    pos = pos_ref[...]

    m_ref[...] = jnp.full(m_ref.shape, -jnp.inf, F32)
    l_ref[...] = jnp.zeros_like(l_ref)
    acc_ref[...] = jnp.zeros_like(acc_ref)

    def block(ki, mask):
        k_blk = k_ref[pl.ds(pl.multiple_of(ki * tk, tk), tk), :]
        k_aug = jnp.concatenate([k_blk, pos], axis=1)
        v_blk = vt_ref[ki]
        c_blk = slope_row * ((ki - qi) * tk).astype(F32)
        for j in range(2):
            s = _dot(k_aug, q_aug[j])
            if mask is not None:
                s = jnp.where(mask, s, -jnp.inf)
            m_old = m_ref[j]
            m_new = jnp.maximum(m_old, jnp.max(s, axis=0, keepdims=True) + c_blk)
            p = jnp.exp(s - (m_new - c_blk))
            alpha = jnp.exp(m_old - m_new)
            l_ref[j] = alpha * l_ref[j] + jnp.sum(p, axis=0, keepdims=True)
            acc_ref[j] = alpha * acc_ref[j] + _dot(v_blk, p.astype(BF16))
            m_ref[j] = m_new

    def body(ki, carry):
        block(ki, None)
        return carry

    lax.fori_loop(0, qi, body, 0)
    kk = lax.broadcasted_iota(jnp.int32, (tk, tq), 0)
    qq = lax.broadcasted_iota(jnp.int32, (tk, tq), 1)
    block(qi, kk <= qq)

    lv = lam_ref[...]
    lam = (jnp.exp(jnp.sum(lv[0:1] * lv[1:2], axis=-1, keepdims=True))
           - jnp.exp(jnp.sum(lv[2:3] * lv[3:4], axis=-1, keepdims=True)) + lambda_init)
    o = acc_ref[0] / l_ref[0] - lam * (acc_ref[1] / l_ref[1])
    g_col = jnp.concatenate([g_ref[...]] * (tq // LANES), axis=1)
    o = o * lax.rsqrt(jnp.mean(o * o, axis=0, keepdims=True) + RMS_EPS) * g_col
    o_ref[...] = (o * (1.0 - lambda_init)).T.astype(BF16)


def _attention(qt, k, vt, pos, slopes, lam_vecs, g_wide, lambda_init, batch, seq):
    t = k.shape[0]
    tq = qt.shape[2]
    nq = seq // tq
    return pl.pallas_call(
        functools.partial(_attn_kernel, lambda_init),
        grid=(batch, ATTN_HEADS, nq),
        in_specs=[
            pl.BlockSpec((1, ATTN_V_DIM, tq), lambda b, h, q: (b * nq + q, h, 0)),
            pl.BlockSpec((seq, ATTN_V_DIM), lambda b, h, q: (b, h)),
            pl.BlockSpec((nq, ATTN_V_DIM, tq), lambda b, h, q: (b, h, 0)),
            _resident((tq, LANES)),
            pl.BlockSpec((1, SUBLANES, LANES), lambda b, h, q: (h, 0, 0)),
            _resident((SUBLANES, LANES)),
            _resident((ATTN_V_DIM, LANES)),
        ],
        out_specs=pl.BlockSpec((tq, ATTN_V_DIM), lambda b, h, q: (b * nq + q, h)),
        out_shape=jax.ShapeDtypeStruct((t, D_MODEL), BF16),
        scratch_shapes=[
            pltpu.VMEM((2, 1, tq), F32),
            pltpu.VMEM((2, 1, tq), F32),
            pltpu.VMEM((2, ATTN_V_DIM, tq), F32),
        ],
        compiler_params=_params(3),
        name="diff_attention",
    )(qt, k, vt, pos, slopes, lam_vecs, g_wide)


def _ffn_kernel(tiles_per_seq, fc, x_ref, gpre_ref, wup_ref, cw_ref, cb_ref, wdn_ref, gpost_ref,
                o_ref, act_ref, carry_ref):
    tm = x_ref.shape[0]

    @pl.when(pl.program_id(0) % tiles_per_seq == 0)
    def _():
        carry_ref[...] = jnp.zeros_like(carry_ref)

    x = x_ref[...]
    hn = _rms(x, gpre_ref[...]).astype(BF16)
    for c in range(D_FF // fc):
        halves = []
        for part in range(2):
            cols = slice(part * D_FF + c * fc, part * D_FF + (c + 1) * fc)
            h = _dot(hn, wup_ref[:, cols])
            prev = carry_ref[:, cols]
            carry_ref[:, cols] = h[tm - SUBLANES:tm, :]
            halves.append(_causal_conv(h, prev, cw_ref, cb_ref, cols, FFN_CONV))
        act_ref[:, c * fc:(c + 1) * fc] = (_silu(halves[0]) * halves[1]).astype(BF16)
    f = _dot(act_ref[...], wdn_ref[...])
    o_ref[...] = x + _rms(f, gpost_ref[...])


def _ffn(x, gpre, wup, conv_w, conv_b, wdn, gpost, seq):
    t = x.shape[0]
    tm = min(ROW_TILE, seq)
    fc = 256
    row = lambda i: (i, 0)
    return pl.pallas_call(
        functools.partial(_ffn_kernel, seq // tm, fc),
        grid=(t // tm,),
        in_specs=[
            pl.BlockSpec((tm, D_MODEL), row),
            _resident((1, D_MODEL)),
            _resident((D_MODEL, 2 * D_FF)),
            _resident((FFN_CONV, 2 * D_FF)),
            _resident((1, 2 * D_FF)),
            _resident((D_FF, D_MODEL)),
            _resident((1, D_MODEL)),
        ],
        out_specs=pl.BlockSpec((tm, D_MODEL), row),
        out_shape=jax.ShapeDtypeStruct((t, D_MODEL), F32),
        scratch_shapes=[
            pltpu.VMEM((tm, D_FF), BF16),
            pltpu.VMEM((SUBLANES, 2 * D_FF), F32),
        ],
        compiler_params=_params(1),
        name="conv_ffn",
    )(x, gpre, wup, conv_w, conv_b, wdn, gpost)


def _row(v):
    return v.astype(F32).reshape(1, -1)


def _pad_lanes(v):
    v = v.astype(F32).reshape(1, -1)
    return jnp.pad(v, ((0, 0), (0, LANES - v.shape[1])))


def _head_expand_matrix():
    head_of_lane = jnp.arange(D_INNER) // SSM_HEAD_DIM
    return (jnp.arange(LANES)[:, None] == head_of_lane[None, :]).astype(BF16)


def _alibi_features(tk):
    kk = jnp.arange(tk)
    cols = jnp.stack([kk // 16, kk % 16, kk // 16, kk % 16], axis=1).astype(F32)
    return jnp.pad(cols, ((0, 0), (0, LANES - 4))).astype(BF16)


def kernel(x, ssm_w_in, ssm_conv_w, ssm_conv_b, ssm_dt_bias, ssm_A_log, ssm_D, ssm_norm_g, ssm_w_out, attn_w_qkv, attn_lambda_q1, attn_lambda_k1, attn_lambda_q2, attn_lambda_k2, attn_subln_g, attn_w_out, norm_mix_pre, norm_mix_post, norm_ffn_pre, norm_ffn_post, ffn_w_up, ffn_conv_w, ffn_conv_b, ffn_w_down):
    batch, seq, d_model = x.shape
    depth = norm_mix_pre.shape[0]
    assert d_model == D_MODEL and seq % ROW_TILE == 0 and seq % ATTN_TILE == 0
    t = batch * seq
    xs = x.reshape(t, D_MODEL).astype(F32)

    expand = _head_expand_matrix()
    tq = min(ATTN_TILE, seq)
    pos = _alibi_features(tq)
    slopes = jnp.exp2(-8.0 * jnp.arange(1, ATTN_HEADS + 1, dtype=F32) / ATTN_HEADS)
    slopes = jnp.broadcast_to(slopes[:, None, None], (ATTN_HEADS, SUBLANES, LANES))

    for i in range(depth):
        j = i // 2
        if i % 2 == 0:
            w_in = ssm_w_in[j]
            wz = w_in[:, :D_INNER].astype(BF16)
            wx = w_in[:, D_INNER:D_INNER + SSM_CONV_DIM].astype(BF16)
            wdt = jnp.pad(w_in[:, D_INNER + SSM_CONV_DIM:],
                          ((0, 0), (0, LANES - SSM_HEADS))).astype(BF16)
            z, xbc, dt = _ssm_inproj(xs, _row(norm_mix_pre[i]), wz, wx, wdt,
                                     ssm_conv_w[j].astype(F32), _row(ssm_conv_b[j]), seq)
            y = _ssd(xbc, z, dt, _pad_lanes(ssm_dt_bias[j]), _pad_lanes(ssm_A_log[j]),
                     _row(jnp.repeat(ssm_D[j], SSM_HEAD_DIM)), _row(ssm_norm_g[j]), expand,
                     batch, seq)
            xs = _out_proj(y, ssm_w_out[j].astype(BF16), _row(norm_mix_post[i]), xs, seq)
        else:
            lambda_init = 0.8 - 0.6 * math.exp(-0.3 * i)
            w_qkv = attn_w_qkv[j]
            wqt = w_qkv[:, :D_MODEL].T.astype(BF16)
            wk = w_qkv[:, D_MODEL:2 * D_MODEL].astype(BF16)
            wvt = w_qkv[:, 2 * D_MODEL:].T.astype(BF16)
            qt, k, vt = _qkv(xs, _row(norm_mix_pre[i]), wqt, wk, wvt, seq)
            lam_vecs = jnp.stack([attn_lambda_q1[j], attn_lambda_k1[j],
                                  attn_lambda_q2[j], attn_lambda_k2[j]]).astype(F32)
            lam_vecs = jnp.pad(lam_vecs, ((0, SUBLANES - 4), (0, LANES - ATTN_HEAD_DIM)))
            g_wide = jnp.broadcast_to(attn_subln_g[j].astype(F32)[:, None], (ATTN_V_DIM, LANES))
            a = _attention(qt, k, vt, pos, slopes, lam_vecs, g_wide, lambda_init, batch, seq)
            xs = _out_proj(a, attn_w_out[j].astype(BF16), _row(norm_mix_post[i]), xs, seq)
        xs = _ffn(xs, _row(norm_ffn_pre[i]), ffn_w_up[i].astype(BF16), ffn_conv_w[i].astype(F32),
                  _row(ffn_conv_b[i]), ffn_w_down[i].astype(BF16), _row(norm_ffn_post[i]), seq)
    return xs.reshape(batch, seq, D_MODEL).astype(x.dtype)
```

```python
import functools
import math

import jax
import jax.numpy as jnp
from jax import lax
from jax.experimental import pallas as pl
from jax.experimental.pallas import tpu as pltpu

F32 = jnp.float32
BF16 = jnp.bfloat16

D_MODEL = 1024
D_INNER = 2048
SSM_HEAD_DIM = 64
SSM_HEADS = 32
SSM_GROUPS = 8
SSM_STATE = 128
SSM_CONV = 4
SSM_CHUNK = 128
SSM_BC_DIM = SSM_GROUPS * SSM_STATE
SSM_CONV_DIM = D_INNER + 2 * SSM_BC_DIM
HEADS_PER_GROUP = SSM_HEADS // SSM_GROUPS
GROUP_WIDTH = D_INNER // SSM_GROUPS
ATTN_HEAD_DIM = 64
ATTN_HEADS = 8
ATTN_V_DIM = 128
D_FF = 2816
FFN_CONV = 3
RMS_EPS = 1e-6
LOG2_E = math.log2(math.e)
POS_DIGIT_BASE = 16.0

LANES = 128
SUBLANES = 8
V7X_VMEM_LIMIT_BYTES = 56 * 1024 * 1024

ROW_TILE = 512
ATTN_TILE = 512
N_CHUNK = 512
FFN_CHUNK = 256


def _params(n_axes):
    return pltpu.CompilerParams(
        dimension_semantics=("arbitrary",) * n_axes,
        vmem_limit_bytes=V7X_VMEM_LIMIT_BYTES,
    )


def _resident(shape):
    zeros = (0,) * len(shape)
    return pl.BlockSpec(shape, lambda *_: zeros, pipeline_mode=pl.Buffered(1))


def _rms(x, g):
    return x * lax.rsqrt(jnp.mean(x * x, axis=-1, keepdims=True) + RMS_EPS) * g


def _silu(x):
    return x / (1.0 + jnp.exp(-x))


def _dot(a, b):
    return jnp.dot(a, b, preferred_element_type=F32)


def _dot_nt(a, b):
    return lax.dot_general(a, b, (((1,), (1,)), ((), ())), preferred_element_type=F32)


def _shift_rows(h, prev, k):
    r = pltpu.roll(h, k, 0)
    pr = pltpu.roll(prev, k, 0)
    row = lax.broadcasted_iota(jnp.int32, pr.shape, 0)
    top = jnp.where(row < k, pr, r[0:SUBLANES])
    return jnp.concatenate([top, r[SUBLANES:]], axis=0)


def _causal_conv(h, prev, w_ref, b_ref, cols, width):
    y = h * w_ref[width - 1:width, cols] + b_ref[:, cols]
    for k in range(1, width):
        y = y + _shift_rows(h, prev, k) * w_ref[width - 1 - k:width - k, cols]
    return y


def _ssm_inproj_kernel(tiles_per_seq, x_ref, g_ref, wz_ref, wx_ref, wdt_ref, cw_ref, cb_ref,
                       z_ref, xbc_ref, dt_ref, carry_ref):
    tm = x_ref.shape[0]

    @pl.when(pl.program_id(0) % tiles_per_seq == 0)
    def _():
        carry_ref[...] = jnp.zeros_like(carry_ref)

    hn = _rms(x_ref[...], g_ref[...]).astype(BF16)
    dt_ref[...] = _dot(hn, wdt_ref[...])
    for c in range(D_INNER // N_CHUNK):
        cols = slice(c * N_CHUNK, (c + 1) * N_CHUNK)
        z_ref[:, cols] = _dot(hn, wz_ref[:, cols]).astype(BF16)
    for c in range(SSM_CONV_DIM // N_CHUNK):
        cols = slice(c * N_CHUNK, (c + 1) * N_CHUNK)
        h = _dot(hn, wx_ref[:, cols])
        prev = carry_ref[:, cols]
        carry_ref[:, cols] = h[tm - SUBLANES:tm, :]
        y = _causal_conv(h, prev, cw_ref, cb_ref, cols, SSM_CONV)
        xbc_ref[:, cols] = _silu(y).astype(BF16)


def _ssm_inproj(x, g, wz, wx, wdt, conv_w, conv_b, seq):
    t = x.shape[0]
    tm = min(ROW_TILE, seq)
    row = lambda i: (i, 0)
    return pl.pallas_call(
        functools.partial(_ssm_inproj_kernel, seq // tm),
        grid=(t // tm,),
        in_specs=[
            pl.BlockSpec((tm, D_MODEL), row),
            _resident((1, D_MODEL)),
            _resident((D_MODEL, D_INNER)),
            _resident((D_MODEL, SSM_CONV_DIM)),
            _resident((D_MODEL, LANES)),
            _resident((SSM_CONV, SSM_CONV_DIM)),
            _resident((1, SSM_CONV_DIM)),
        ],
        out_specs=[
            pl.BlockSpec((tm, D_INNER), row),
            pl.BlockSpec((tm, SSM_CONV_DIM), row),
            pl.BlockSpec((tm, LANES), row),
        ],
        out_shape=[
            jax.ShapeDtypeStruct((t, D_INNER), BF16),
            jax.ShapeDtypeStruct((t, SSM_CONV_DIM), BF16),
            jax.ShapeDtypeStruct((t, LANES), F32),
        ],
        scratch_shapes=[pltpu.VMEM((SUBLANES, SSM_CONV_DIM), F32)],
        compiler_params=_params(1),
        name="ssm_inproj",
    )(x, g, wz, wx, wdt, conv_w, conv_b)


def _ssd_kernel(xbc_ref, z_ref, dt_ref, dtb_ref, alog_ref, dexp_ref, ng_ref, e_ref,
                y_ref, state_ref):
    L = SSM_CHUNK

    @pl.when(pl.program_id(1) == 0)
    def _():
        state_ref[...] = jnp.zeros_like(state_ref)

    dt_raw = dt_ref[...] + dtb_ref[...]
    dt = jnp.maximum(dt_raw, 0.0) + jnp.log1p(jnp.exp(-jnp.abs(dt_raw)))
    d_a = dt * (-jnp.exp(alog_ref[...]))
    row = lax.broadcasted_iota(jnp.int32, (L, LANES), 0)
    lane = lax.broadcasted_iota(jnp.int32, (L, LANES), 1)
    acs = d_a
    k = 1
    while k < L:
        acs = acs + jnp.where(row >= k, pltpu.roll(acs, k, 0), 0.0)
        k *= 2
    acs_t = acs.T
    dt_t = dt.T
    w_t = dt_t * jnp.exp(acs_t[:, L - 1:L] - acs_t)
    exp_acs = jnp.exp(acs)

    cd = jnp.broadcast_to(exp_acs[L - 1:L, :], (SUBLANES, LANES))
    cd_hi = cd.astype(BF16)
    cd_lo = (cd - cd_hi.astype(F32)).astype(BF16)
    cd_wide = (_dot(cd_hi, e_ref[...]) + _dot(cd_lo, e_ref[...]))[0:1, :]

    causal = row >= lane
    left = lane < SSM_HEAD_DIM

    for g in range(SSM_GROUPS):
        b_g = xbc_ref[:, D_INNER + g * SSM_STATE:D_INNER + (g + 1) * SSM_STATE]
        c_g = xbc_ref[:, D_INNER + SSM_BC_DIM + g * SSM_STATE:
                      D_INNER + SSM_BC_DIM + (g + 1) * SSM_STATE]
        cb = _dot_nt(c_g, b_g)
        b_t = b_g.astype(F32).T
        c_f = c_g.astype(F32)
        ys = []
        for pr in range(HEADS_PER_GROUP // 2):
            heads = (g * HEADS_PER_GROUP + 2 * pr, g * HEADS_PER_GROUP + 2 * pr + 1)
            cols = slice((g * 2 + pr) * LANES, (g * 2 + pr + 1) * LANES)
            scols = slice(pr * LANES, (pr + 1) * LANES)
            xp = xbc_ref[:, cols].astype(F32)
            sp = state_ref[g, :, scols]
            m_parts, c_parts, bw_parts = [], [], []
            for h in heads:
                a_col = jnp.broadcast_to(acs[:, h:h + 1], (L, L))
                a_row = jnp.broadcast_to(acs_t[h:h + 1, :], (L, L))
                decay = jnp.exp(jnp.where(causal, a_col - a_row, -jnp.inf))
                m_parts.append((cb * decay * jnp.broadcast_to(dt_t[h:h + 1, :], (L, L))).astype(BF16))
                c_parts.append((c_f * jnp.broadcast_to(exp_acs[:, h:h + 1], (L, L))).astype(BF16))
                bw_parts.append((b_t * jnp.broadcast_to(w_t[h:h + 1, :], (L, L))).astype(BF16))
            x_l = jnp.where(left, xp, 0.0).astype(BF16)
            x_r = jnp.where(left, 0.0, xp).astype(BF16)
            s_l = jnp.where(left, sp, 0.0).astype(BF16)
            s_r = jnp.where(left, 0.0, sp).astype(BF16)
            x_blk = jnp.concatenate([x_l, x_r], axis=0)
            y_p = _dot(jnp.concatenate(m_parts + c_parts, axis=1),
                       jnp.concatenate([x_blk, s_l, s_r], axis=0))
            y_p = y_p + xp * dexp_ref[:, cols]
            state_ref[g, :, scols] = sp * cd_wide[:, cols] + _dot(
                jnp.concatenate(bw_parts, axis=1), x_blk)
            ys.append(y_p)
        gcols = slice(g * GROUP_WIDTH, (g + 1) * GROUP_WIDTH)
        y_g = jnp.concatenate(ys, axis=1) * _silu(z_ref[:, gcols].astype(F32))
        y_ref[:, gcols] = _rms(y_g, ng_ref[:, gcols]).astype(BF16)


def _ssd(xbc, z, dt, dt_bias, a_log, d_wide, norm_g, expand, batch, seq):
    t = xbc.shape[0]
    n_chunks = seq // SSM_CHUNK
    row = lambda b, c: (b * n_chunks + c, 0)
    return pl.pallas_call(
        _ssd_kernel,
        grid=(batch, n_chunks),
        in_specs=[
            pl.BlockSpec((SSM_CHUNK, SSM_CONV_DIM), row),
            pl.BlockSpec((SSM_CHUNK, D_INNER), row),
            pl.BlockSpec((SSM_CHUNK, LANES), row),
            _resident((1, LANES)),
            _resident((1, LANES)),
            _resident((1, D_INNER)),
            _resident((1, D_INNER)),
            _resident((LANES, D_INNER)),
        ],
        out_specs=pl.BlockSpec((SSM_CHUNK, D_INNER), row),
        out_shape=jax.ShapeDtypeStruct((t, D_INNER), BF16),
        scratch_shapes=[pltpu.VMEM((SSM_GROUPS, SSM_STATE, GROUP_WIDTH), F32)],
        compiler_params=_params(2),
        name="ssd_scan",
    )(xbc, z, dt, dt_bias, a_log, d_wide, norm_g, expand)


def _out_proj_kernel(y_ref, w_ref, g_ref, x_ref, o_ref):
    m = _dot(y_ref[...], w_ref[...])
    o_ref[...] = x_ref[...] + _rms(m, g_ref[...])


def _out_proj(y, w, g, x, seq):
    t, k = y.shape
    tm = min(ROW_TILE, seq)
    row = lambda i: (i, 0)
    return pl.pallas_call(
        _out_proj_kernel,
        grid=(t // tm,),
        in_specs=[
            pl.BlockSpec((tm, k), row),
            _resident((k, D_MODEL)),
            _resident((1, D_MODEL)),
            pl.BlockSpec((tm, D_MODEL), row),
        ],
        out_specs=pl.BlockSpec((tm, D_MODEL), row),
        out_shape=jax.ShapeDtypeStruct((t, D_MODEL), F32),
        compiler_params=_params(1),
        name="mixer_out_proj",
    )(y, w, g, x)


def _qkv_kernel(x_ref, g_ref, wqt_ref, wk_ref, wvt_ref, qt_ref, k_ref, vt_ref):
    hn = _rms(x_ref[...], g_ref[...]).astype(BF16)
    scale = ATTN_HEAD_DIM ** -0.5 * LOG2_E
    for c in range(D_MODEL // N_CHUNK):
        rows = slice(c * N_CHUNK, (c + 1) * N_CHUNK)
        qt_ref[0, rows, :] = (_dot_nt(wqt_ref[rows, :], hn) * scale).astype(BF16)
        vt_ref[0, rows, :] = _dot_nt(wvt_ref[rows, :], hn).astype(BF16)
        k_ref[:, rows] = _dot(hn, wk_ref[:, rows]).astype(BF16)


def _qkv(x, g, wqt, wk, wvt, seq):
    t = x.shape[0]
    tm = min(ATTN_TILE, seq)
    row = lambda i: (i, 0)
    tile = lambda i: (i, 0, 0)
    return pl.pallas_call(
        _qkv_kernel,
        grid=(t // tm,),
        in_specs=[
            pl.BlockSpec((tm, D_MODEL), row),
            _resident((1, D_MODEL)),
            _resident((D_MODEL, D_MODEL)),
            _resident((D_MODEL, D_MODEL)),
            _resident((D_MODEL, D_MODEL)),
        ],
        out_specs=[
            pl.BlockSpec((1, D_MODEL, tm), tile),
            pl.BlockSpec((tm, D_MODEL), row),
            pl.BlockSpec((1, D_MODEL, tm), tile),
        ],
        out_shape=[
            jax.ShapeDtypeStruct((t // tm, D_MODEL, tm), BF16),
            jax.ShapeDtypeStruct((t, D_MODEL), BF16),
            jax.ShapeDtypeStruct((t // tm, D_MODEL, tm), BF16),
        ],
        compiler_params=_params(1),
        name="attn_qkv",
    )(x, g, wqt, wk, wvt)


def _attn_kernel(lambda_init, qt_ref, k_ref, vt_ref, pos_ref, slope_ref, lam_ref, g_ref,
                 o_ref, s_ref, m_ref, l_ref, acc_ref):
    tq = qt_ref.shape[2]
    tk = tq
    qi = pl.program_id(2)

    slope = jnp.concatenate([slope_ref[0]] * (tq // LANES), axis=1) * LOG2_E
    s_hi = slope.astype(BF16).astype(F32)
    s_lo = slope - s_hi
    r8 = lax.broadcasted_iota(jnp.int32, (SUBLANES, tq), 0)
    feat = jnp.where(r8 == 0, POS_DIGIT_BASE * s_hi,
                     jnp.where(r8 == 1, s_hi,
                               jnp.where(r8 == 2, POS_DIGIT_BASE * s_lo,
                                         jnp.where(r8 == 3, s_lo, 0.0)))).astype(BF16)
    aug = jnp.concatenate([feat, jnp.zeros((LANES - SUBLANES, tq), BF16)], axis=0)
    zero_half = jnp.zeros((ATTN_HEAD_DIM, tq), BF16)
    q_all = qt_ref[0]
    q_aug = (jnp.concatenate([q_all[0:ATTN_HEAD_DIM], zero_half, aug], axis=0),
             jnp.concatenate([zero_half, q_all[ATTN_HEAD_DIM:], aug], axis=0))
    slope_row = slope[0:1, :]
    pos = pos_ref[...]

    m_ref[...] = jnp.full(m_ref.shape, -jnp.inf, F32)
    l_ref[...] = jnp.zeros_like(l_ref)
    acc_ref[...] = jnp.zeros_like(acc_ref)

    def scores(ki, slot):
        k_blk = k_ref[pl.ds(pl.multiple_of(ki * tk, tk), tk), :]
        k_aug = jnp.concatenate([k_blk, pos], axis=1)
        for j in range(2):
            s_ref[slot, j] = _dot(k_aug, q_aug[j])

    def softmax_pv(ki, slot, mask):
        v_blk = vt_ref[ki]
        c_blk = slope_row * ((ki - qi) * tk).astype(F32)
        for j in range(2):
            s = s_ref[slot, j]
            if mask is not None:
                s = jnp.where(mask, s, -jnp.inf)
            m_old = m_ref[j]
            m_new = jnp.maximum(m_old, jnp.max(s, axis=0, keepdims=True) + c_blk)
            p = jnp.exp2(s - (m_new - c_blk))
            alpha = jnp.exp2(m_old - m_new)
            l_ref[j] = alpha * l_ref[j] + jnp.sum(p, axis=0, keepdims=True)
            acc_ref[j] = alpha * acc_ref[j] + _dot(v_blk, p.astype(BF16))
            m_ref[j] = m_new

    scores(0, 0)

    def body(i, carry):
        b0 = 2 * i
        scores(b0 + 1, 1)
        softmax_pv(b0, 0, None)
        scores(b0 + 2, 0)
        softmax_pv(b0 + 1, 1, None)
        return carry

    lax.fori_loop(0, qi // 2, body, 0)
    kk = lax.broadcasted_iota(jnp.int32, (tk, tq), 0)
    qq = lax.broadcasted_iota(jnp.int32, (tk, tq), 1)
    causal = kk <= qq

    @pl.when(qi % 2 == 0)
    def _():
        softmax_pv(qi, 0, causal)

    @pl.when(qi % 2 == 1)
    def _():
        scores(qi, 1)
        softmax_pv(qi - 1, 0, None)
        softmax_pv(qi, 1, causal)

    lv = lam_ref[...]
    lam = (jnp.exp(jnp.sum(lv[0:1] * lv[1:2], axis=-1, keepdims=True))
           - jnp.exp(jnp.sum(lv[2:3] * lv[3:4], axis=-1, keepdims=True)) + lambda_init)
    o = acc_ref[0] / l_ref[0] - lam * (acc_ref[1] / l_ref[1])
    g_col = jnp.concatenate([g_ref[...]] * (tq // LANES), axis=1)
    o = o * lax.rsqrt(jnp.mean(o * o, axis=0, keepdims=True) + RMS_EPS) * g_col
    o_ref[...] = (o * (1.0 - lambda_init)).T.astype(BF16)


def _attention(qt, k, vt, pos, slopes, lam_vecs, g_wide, lambda_init, batch, seq):
    t = k.shape[0]
    tq = qt.shape[2]
    nq = seq // tq
    return pl.pallas_call(
        functools.partial(_attn_kernel, lambda_init),
        grid=(batch, ATTN_HEADS, nq),
        in_specs=[
            pl.BlockSpec((1, ATTN_V_DIM, tq), lambda b, h, q: (b * nq + q, h, 0)),
            pl.BlockSpec((seq, ATTN_V_DIM), lambda b, h, q: (b, h)),
            pl.BlockSpec((nq, ATTN_V_DIM, tq), lambda b, h, q: (b, h, 0)),
            _resident((tq, LANES)),
            pl.BlockSpec((1, SUBLANES, LANES), lambda b, h, q: (h, 0, 0)),
            _resident((SUBLANES, LANES)),
            _resident((ATTN_V_DIM, LANES)),
        ],
        out_specs=pl.BlockSpec((tq, ATTN_V_DIM), lambda b, h, q: (b * nq + q, h)),
        out_shape=jax.ShapeDtypeStruct((t, D_MODEL), BF16),
        scratch_shapes=[
            pltpu.VMEM((2, 2, tq, tq), F32),
            pltpu.VMEM((2, 1, tq), F32),
            pltpu.VMEM((2, 1, tq), F32),
            pltpu.VMEM((2, ATTN_V_DIM, tq), F32),
        ],
        compiler_params=_params(3),
        name="diff_attention",
    )(qt, k, vt, pos, slopes, lam_vecs, g_wide)


def _ffn_kernel(tiles_per_seq, fc, x_ref, gpre_ref, wup_ref, cw_ref, cb_ref, wdn_ref, gpost_ref,
                o_ref, act_ref, carry_ref):
    tm = x_ref.shape[0]

    @pl.when(pl.program_id(0) % tiles_per_seq == 0)
    def _():
        carry_ref[...] = jnp.zeros_like(carry_ref)

    x = x_ref[...]
    hn = _rms(x, gpre_ref[...]).astype(BF16)
    for c in range(D_FF // fc):
        halves = []
        for part in range(2):
            cols = slice(part * D_FF + c * fc, part * D_FF + (c + 1) * fc)
            h = _dot(hn, wup_ref[:, cols])
            prev = carry_ref[:, cols]
            carry_ref[:, cols] = h[tm - SUBLANES:tm, :]
            halves.append(_causal_conv(h, prev, cw_ref, cb_ref, cols, FFN_CONV))
        act_ref[:, c * fc:(c + 1) * fc] = (_silu(halves[0]) * halves[1]).astype(BF16)
    f = _dot(act_ref[...], wdn_ref[...])
    o_ref[...] = x + _rms(f, gpost_ref[...])


def _ffn(x, gpre, wup, conv_w, conv_b, wdn, gpost, seq):
    t = x.shape[0]
    tm = min(ROW_TILE, seq)
    fc = FFN_CHUNK
    row = lambda i: (i, 0)
    return pl.pallas_call(
        functools.partial(_ffn_kernel, seq // tm, fc),
        grid=(t // tm,),
        in_specs=[
            pl.BlockSpec((tm, D_MODEL), row),
            _resident((1, D_MODEL)),
            _resident((D_MODEL, 2 * D_FF)),
            _resident((FFN_CONV, 2 * D_FF)),
            _resident((1, 2 * D_FF)),
            _resident((D_FF, D_MODEL)),
            _resident((1, D_MODEL)),
        ],
        out_specs=pl.BlockSpec((tm, D_MODEL), row),
        out_shape=jax.ShapeDtypeStruct((t, D_MODEL), F32),
        scratch_shapes=[
            pltpu.VMEM((tm, D_FF), BF16),
            pltpu.VMEM((SUBLANES, 2 * D_FF), F32),
        ],
        compiler_params=_params(1),
        name="conv_ffn",
    )(x, gpre, wup, conv_w, conv_b, wdn, gpost)


def _row(v):
    return v.astype(F32).reshape(1, -1)


def _pad_lanes(v):
    v = v.astype(F32).reshape(1, -1)
    return jnp.pad(v, ((0, 0), (0, LANES - v.shape[1])))


def _head_expand_matrix():
    head_of_lane = jnp.arange(D_INNER) // SSM_HEAD_DIM
    return (jnp.arange(LANES)[:, None] == head_of_lane[None, :]).astype(BF16)


def _alibi_features(tk):
    kk = jnp.arange(tk)
    base = int(POS_DIGIT_BASE)
    cols = jnp.stack([kk // base, kk % base, kk // base, kk % base], axis=1).astype(F32)
    return jnp.pad(cols, ((0, 0), (0, LANES - 4))).astype(BF16)


def kernel(x, ssm_w_in, ssm_conv_w, ssm_conv_b, ssm_dt_bias, ssm_A_log, ssm_D, ssm_norm_g, ssm_w_out, attn_w_qkv, attn_lambda_q1, attn_lambda_k1, attn_lambda_q2, attn_lambda_k2, attn_subln_g, attn_w_out, norm_mix_pre, norm_mix_post, norm_ffn_pre, norm_ffn_post, ffn_w_up, ffn_conv_w, ffn_conv_b, ffn_w_down):
    batch, seq, d_model = x.shape
    depth = norm_mix_pre.shape[0]
    assert d_model == D_MODEL and seq % ROW_TILE == 0 and seq % ATTN_TILE == 0
    t = batch * seq
    xs = x.reshape(t, D_MODEL).astype(F32)

    expand = _head_expand_matrix()
    tq = min(ATTN_TILE, seq)
    pos = _alibi_features(tq)
    slopes = jnp.exp2(-8.0 * jnp.arange(1, ATTN_HEADS + 1, dtype=F32) / ATTN_HEADS)
    slopes = jnp.broadcast_to(slopes[:, None, None], (ATTN_HEADS, SUBLANES, LANES))

    for i in range(depth):
        j = i // 2
        if i % 2 == 0:
            w_in = ssm_w_in[j]
            wz = w_in[:, :D_INNER].astype(BF16)
            wx = w_in[:, D_INNER:D_INNER + SSM_CONV_DIM].astype(BF16)
            wdt = jnp.pad(w_in[:, D_INNER + SSM_CONV_DIM:],
                          ((0, 0), (0, LANES - SSM_HEADS))).astype(BF16)
            z, xbc, dt = _ssm_inproj(xs, _row(norm_mix_pre[i]), wz, wx, wdt,
                                     ssm_conv_w[j].astype(F32), _row(ssm_conv_b[j]), seq)
            y = _ssd(xbc, z, dt, _pad_lanes(ssm_dt_bias[j]), _pad_lanes(ssm_A_log[j]),
                     _row(jnp.repeat(ssm_D[j], SSM_HEAD_DIM)), _row(ssm_norm_g[j]), expand,
                     batch, seq)
            xs = _out_proj(y, ssm_w_out[j].astype(BF16), _row(norm_mix_post[i]), xs, seq)
        else:
            lambda_init = 0.8 - 0.6 * math.exp(-0.3 * i)
            w_qkv = attn_w_qkv[j]
            wqt = w_qkv[:, :D_MODEL].T.astype(BF16)
            wk = w_qkv[:, D_MODEL:2 * D_MODEL].astype(BF16)
            wvt = w_qkv[:, 2 * D_MODEL:].T.astype(BF16)
            qt, k, vt = _qkv(xs, _row(norm_mix_pre[i]), wqt, wk, wvt, seq)
            lam_vecs = jnp.stack([attn_lambda_q1[j], attn_lambda_k1[j],
                                  attn_lambda_q2[j], attn_lambda_k2[j]]).astype(F32)
            lam_vecs = jnp.pad(lam_vecs, ((0, SUBLANES - 4), (0, LANES - ATTN_HEAD_DIM)))
            g_wide = jnp.broadcast_to(attn_subln_g[j].astype(F32)[:, None], (ATTN_V_DIM, LANES))
            a = _attention(qt, k, vt, pos, slopes, lam_vecs, g_wide, lambda_init, batch, seq)
            xs = _out_proj(a, attn_w_out[j].astype(BF16), _row(norm_mix_post[i]), xs, seq)
        xs = _ffn(xs, _row(norm_ffn_pre[i]), ffn_w_up[i].astype(BF16), ffn_conv_w[i].astype(F32),
                  _row(ffn_conv_b[i]), ffn_w_down[i].astype(BF16), _row(norm_ffn_post[i]), seq)
    return xs.reshape(batch, seq, D_MODEL).astype(x.dtype)
```

```python
import functools
import math

import jax
import jax.numpy as jnp
from jax import lax
from jax.experimental import pallas as pl
from jax.experimental.pallas import tpu as pltpu

F32 = jnp.float32
BF16 = jnp.bfloat16

D_MODEL = 1024
D_INNER = 2048
SSM_HEAD_DIM = 64
SSM_HEADS = 32
SSM_GROUPS = 8
SSM_STATE = 128
SSM_CONV = 4
SSM_CHUNK = 128
SSM_BC_DIM = SSM_GROUPS * SSM_STATE
SSM_CONV_DIM = D_INNER + 2 * SSM_BC_DIM
HEADS_PER_GROUP = SSM_HEADS // SSM_GROUPS
GROUP_WIDTH = D_INNER // SSM_GROUPS
ATTN_HEAD_DIM = 64
ATTN_HEADS = 8
ATTN_V_DIM = 128
D_FF = 2816
FFN_CONV = 3
RMS_EPS = 1e-6
LOG2_E = math.log2(math.e)
POS_DIGIT_BASE = 16.0

LANES = 128
SUBLANES = 8
V7X_VMEM_LIMIT_BYTES = 56 * 1024 * 1024

ROW_TILE = 512
ATTN_TILE = 512
N_CHUNK = 512
FFN_CHUNK = 256
SSD_CHUNKS_PER_STEP = 4


def _params(n_axes):
    return pltpu.CompilerParams(
        dimension_semantics=("arbitrary",) * n_axes,
        vmem_limit_bytes=V7X_VMEM_LIMIT_BYTES,
    )


def _resident(shape):
    zeros = (0,) * len(shape)
    return pl.BlockSpec(shape, lambda *_: zeros, pipeline_mode=pl.Buffered(1))


def _rms(x, g):
    return x * lax.rsqrt(jnp.mean(x * x, axis=-1, keepdims=True) + RMS_EPS) * g


def _silu(x):
    half = 0.5 * x
    return half + half * jnp.tanh(half)


def _dot(a, b):
    return jnp.dot(a, b, preferred_element_type=F32)


def _dot_nt(a, b):
    return lax.dot_general(a, b, (((1,), (1,)), ((), ())), preferred_element_type=F32)


def _causal_conv(h, carry_ref, w_ref, b_ref, cols, width):
    rows, c = h.shape
    prev = carry_ref[:, cols]
    carry_ref[:, cols] = h[rows - SUBLANES:rows, :]
    tiles = jnp.concatenate([prev, h], axis=0).reshape(rows // SUBLANES + 1, SUBLANES, c)
    sub = lax.broadcasted_iota(jnp.int32, (rows // SUBLANES, SUBLANES, c), 1)
    y = h * w_ref[width - 1:width, cols] + b_ref[:, cols]
    for k in range(1, width):
        rot = pltpu.roll(tiles, k, 1)
        shifted = jnp.where(sub < k, rot[:-1], rot[1:]).reshape(rows, c)
        y = y + shifted * w_ref[width - 1 - k:width - k, cols]
    return y


def _ssm_inproj_kernel(tiles_per_seq, x_ref, g_ref, wz_ref, wx_ref, wdt_ref, cw_ref, cb_ref,
                       z_ref, xbc_ref, dt_ref, carry_ref):
    @pl.when(pl.program_id(0) % tiles_per_seq == 0)
    def _():
        carry_ref[...] = jnp.zeros_like(carry_ref)

    hn = _rms(x_ref[...], g_ref[...]).astype(BF16)
    dt_ref[...] = _dot(hn, wdt_ref[...])
    n_chunks = SSM_CONV_DIM // N_CHUNK
    zc = D_INNER // n_chunks
    h_next = _dot(hn, wx_ref[:, 0:N_CHUNK])
    for c in range(n_chunks):
        cols = slice(c * N_CHUNK, (c + 1) * N_CHUNK)
        h = h_next
        if c + 1 < n_chunks:
            h_next = _dot(hn, wx_ref[:, (c + 1) * N_CHUNK:(c + 2) * N_CHUNK])
        zcols = slice(c * zc, (c + 1) * zc)
        z_ref[:, zcols] = _dot(hn, wz_ref[:, zcols]).astype(BF16)
        y = _causal_conv(h, carry_ref, cw_ref, cb_ref, cols, SSM_CONV)
        xbc_ref[:, cols] = _silu(y).astype(BF16)


def _ssm_inproj(x, g, wz, wx, wdt, conv_w, conv_b, seq):
    t = x.shape[0]
    tm = min(ROW_TILE, seq)
    row = lambda i: (i, 0)
    return pl.pallas_call(
        functools.partial(_ssm_inproj_kernel, seq // tm),
        grid=(t // tm,),
        in_specs=[
            pl.BlockSpec((tm, D_MODEL), row),
            _resident((1, D_MODEL)),
            _resident((D_MODEL, D_INNER)),
            _resident((D_MODEL, SSM_CONV_DIM)),
            _resident((D_MODEL, LANES)),
            _resident((SSM_CONV, SSM_CONV_DIM)),
            _resident((1, SSM_CONV_DIM)),
        ],
        out_specs=[
            pl.BlockSpec((tm, D_INNER), row),
            pl.BlockSpec((tm, SSM_CONV_DIM), row),
            pl.BlockSpec((tm, LANES), row),
        ],
        out_shape=[
            jax.ShapeDtypeStruct((t, D_INNER), BF16),
            jax.ShapeDtypeStruct((t, SSM_CONV_DIM), BF16),
            jax.ShapeDtypeStruct((t, LANES), F32),
        ],
        scratch_shapes=[pltpu.VMEM((SUBLANES, SSM_CONV_DIM), F32)],
        compiler_params=_params(1),
        name="ssm_inproj",
    )(x, g, wz, wx, wdt, conv_w, conv_b)


def _ssd_chunk(rows, xbc_ref, z_ref, dt_ref, dtb_ref, alog_ref, dexp_ref, ng_ref, e_ref,
               y_ref, state_ref):
    L = SSM_CHUNK

    dt_raw = dt_ref[rows, :] + dtb_ref[...]
    dt = jnp.maximum(dt_raw, 0.0) + jnp.log(1.0 + jnp.exp(-jnp.abs(dt_raw)))
    row = lax.broadcasted_iota(jnp.int32, (L, LANES), 0)
    lane = lax.broadcasted_iota(jnp.int32, (L, LANES), 1)
    acs = dt * (-jnp.exp(alog_ref[...]) * LOG2_E)
    k = 1
    while k < L:
        acs = acs + jnp.where(row >= k, pltpu.roll(acs, k, 0), 0.0)
        k *= 2
    acs_t = acs.T
    rowp_t = acs_t - jnp.log2(dt.T)
    w_t = jnp.exp2(acs_t[:, L - 1:L] - rowp_t)

    cd = jnp.broadcast_to(jnp.exp2(acs[L - 1:L, :]), (SUBLANES, LANES))
    cd_hi = cd.astype(BF16)
    cd_lo = (cd - cd_hi.astype(F32)).astype(BF16)
    cd_wide = (_dot(cd_hi, e_ref[...]) + _dot(cd_lo, e_ref[...]))[0:1, :]

    causal = row >= lane
    keep_l = jnp.where(lane < SSM_HEAD_DIM, 1.0, 0.0).astype(BF16)
    keep_r = jnp.where(lane < SSM_HEAD_DIM, 0.0, 1.0).astype(BF16)

    for g in range(SSM_GROUPS):
        b_g = xbc_ref[rows, D_INNER + g * SSM_STATE:D_INNER + (g + 1) * SSM_STATE]
        c_g = xbc_ref[rows, D_INNER + SSM_BC_DIM + g * SSM_STATE:
                      D_INNER + SSM_BC_DIM + (g + 1) * SSM_STATE]
        cb = _dot_nt(c_g, b_g)
        b_t = b_g.astype(F32).T
        c_f = c_g.astype(F32)
        ys = []
        for pr in range(HEADS_PER_GROUP // 2):
            heads = (g * HEADS_PER_GROUP + 2 * pr, g * HEADS_PER_GROUP + 2 * pr + 1)
            cols = slice((g * 2 + pr) * LANES, (g * 2 + pr + 1) * LANES)
            scols = slice(pr * LANES, (pr + 1) * LANES)
            xp = xbc_ref[rows, cols]
            sp = state_ref[g, :, scols]
            m_parts, c_parts, bw_parts = [], [], []
            for h in heads:
                a_col = jnp.broadcast_to(acs[:, h:h + 1], (L, L))
                a_row = jnp.broadcast_to(rowp_t[h:h + 1, :], (L, L))
                m_parts.append(
                    (cb * jnp.exp2(jnp.where(causal, a_col - a_row, -jnp.inf))).astype(BF16))
                c_parts.append((c_f * jnp.exp2(a_col)).astype(BF16))
                bw_parts.append((b_t * jnp.broadcast_to(w_t[h:h + 1, :], (L, L))).astype(BF16))
            sb = sp.astype(BF16)
            x_blk = jnp.concatenate([xp * keep_l, xp * keep_r], axis=0)
            y_p = _dot(jnp.concatenate(m_parts + c_parts, axis=1),
                       jnp.concatenate([x_blk, sb * keep_l, sb * keep_r], axis=0))
            y_p = y_p + xp.astype(F32) * dexp_ref[:, cols]
            state_ref[g, :, scols] = sp * cd_wide[:, cols] + _dot(
                jnp.concatenate(bw_parts, axis=1), x_blk)
            ys.append(y_p)
        gcols = slice(g * GROUP_WIDTH, (g + 1) * GROUP_WIDTH)
        y_g = jnp.concatenate(ys, axis=1) * _silu(z_ref[rows, gcols].astype(F32))
        y_ref[rows, gcols] = _rms(y_g, ng_ref[:, gcols]).astype(BF16)


def _ssd_kernel(*refs):
    state_ref = refs[-1]

    @pl.when(pl.program_id(1) == 0)
    def _():
        state_ref[...] = jnp.zeros_like(state_ref)

    for sub in range(SSD_CHUNKS_PER_STEP):
        _ssd_chunk(slice(sub * SSM_CHUNK, (sub + 1) * SSM_CHUNK), *refs)


def _ssd(xbc, z, dt, dt_bias, a_log, d_wide, norm_g, expand, batch, seq):
    t = xbc.shape[0]
    rows = SSD_CHUNKS_PER_STEP * SSM_CHUNK
    n_steps = seq // rows
    row = lambda b, c: (b * n_steps + c, 0)
    return pl.pallas_call(
        _ssd_kernel,
        grid=(batch, n_steps),
        in_specs=[
            pl.BlockSpec((rows, SSM_CONV_DIM), row),
            pl.BlockSpec((rows, D_INNER), row),
            pl.BlockSpec((rows, LANES), row),
            _resident((1, LANES)),
            _resident((1, LANES)),
            _resident((1, D_INNER)),
            _resident((1, D_INNER)),
            _resident((LANES, D_INNER)),
        ],
        out_specs=pl.BlockSpec((rows, D_INNER), row),
        out_shape=jax.ShapeDtypeStruct((t, D_INNER), BF16),
        scratch_shapes=[pltpu.VMEM((SSM_GROUPS, SSM_STATE, GROUP_WIDTH), F32)],
        compiler_params=_params(2),
        name="ssd_scan",
    )(xbc, z, dt, dt_bias, a_log, d_wide, norm_g, expand)


def _qkv_kernel(x_ref, g_ref, wqt_ref, wk_ref, wvt_ref, qt_ref, k_ref, vt_ref):
    hn = _rms(x_ref[...], g_ref[...]).astype(BF16)
    scale = ATTN_HEAD_DIM ** -0.5 * LOG2_E
    for c in range(D_MODEL // N_CHUNK):
        rows = slice(c * N_CHUNK, (c + 1) * N_CHUNK)
        qt_ref[0, rows, :] = (_dot_nt(wqt_ref[rows, :], hn) * scale).astype(BF16)
        vt_ref[0, rows, :] = _dot_nt(wvt_ref[rows, :], hn).astype(BF16)
        k_ref[:, rows] = _dot(hn, wk_ref[:, rows]).astype(BF16)


def _qkv(x, g, wqt, wk, wvt, seq):
    t = x.shape[0]
    tm = min(ATTN_TILE, seq)
    row = lambda i: (i, 0)
    tile = lambda i: (i, 0, 0)
    return pl.pallas_call(
        _qkv_kernel,
        grid=(t // tm,),
        in_specs=[
            pl.BlockSpec((tm, D_MODEL), row),
            _resident((1, D_MODEL)),
            _resident((D_MODEL, D_MODEL)),
            _resident((D_MODEL, D_MODEL)),
            _resident((D_MODEL, D_MODEL)),
        ],
        out_specs=[
            pl.BlockSpec((1, D_MODEL, tm), tile),
            pl.BlockSpec((tm, D_MODEL), row),
            pl.BlockSpec((1, D_MODEL, tm), tile),
        ],
        out_shape=[
            jax.ShapeDtypeStruct((t // tm, D_MODEL, tm), BF16),
            jax.ShapeDtypeStruct((t, D_MODEL), BF16),
            jax.ShapeDtypeStruct((t // tm, D_MODEL, tm), BF16),
        ],
        compiler_params=_params(1),
        name="attn_qkv",
    )(x, g, wqt, wk, wvt)


def _attn_kernel(lambda_init, qt_ref, k_ref, vt_ref, pos_ref, slope_ref, lam_ref, g_ref,
                 o_ref, s_ref, m_ref, l_ref, acc_ref):
    tq = qt_ref.shape[2]
    tk = tq
    qi = pl.program_id(2)

    slope = jnp.concatenate([slope_ref[0]] * (tq // LANES), axis=1) * LOG2_E
    s_hi = slope.astype(BF16).astype(F32)
    s_lo = slope - s_hi
    r8 = lax.broadcasted_iota(jnp.int32, (SUBLANES, tq), 0)
    feat = jnp.where(r8 == 0, POS_DIGIT_BASE * s_hi,
                     jnp.where(r8 == 1, s_hi,
                               jnp.where(r8 == 2, POS_DIGIT_BASE * s_lo,
                                         jnp.where(r8 == 3, s_lo, 0.0)))).astype(BF16)
    aug = jnp.concatenate([feat, jnp.zeros((LANES - SUBLANES, tq), BF16)], axis=0)
    zero_half = jnp.zeros((ATTN_HEAD_DIM, tq), BF16)
    q_all = qt_ref[0]
    q_aug = (jnp.concatenate([q_all[0:ATTN_HEAD_DIM], zero_half, aug], axis=0),
             jnp.concatenate([zero_half, q_all[ATTN_HEAD_DIM:], aug], axis=0))
    slope_row = slope[0:1, :]
    pos = pos_ref[...]

    m_ref[...] = jnp.full(m_ref.shape, -jnp.inf, F32)
    l_ref[...] = jnp.zeros_like(l_ref)
    acc_ref[...] = jnp.zeros_like(acc_ref)

    def scores(ki, slot):
        k_blk = k_ref[pl.ds(pl.multiple_of(ki * tk, tk), tk), :]
        k_aug = jnp.concatenate([k_blk, pos], axis=1)
        for j in range(2):
            s_ref[slot, j] = _dot(k_aug, q_aug[j])

    def softmax_pv(ki, slot, mask):
        v_blk = vt_ref[ki]
        c_blk = slope_row * ((ki - qi) * tk).astype(F32)
        for j in range(2):
            s = s_ref[slot, j]
            if mask is not None:
                s = jnp.where(mask, s, -jnp.inf)
            m_old = m_ref[j]
            m_new = jnp.maximum(m_old, jnp.max(s, axis=0, keepdims=True) + c_blk)
            p = jnp.exp2(s - (m_new - c_blk))
            alpha = jnp.exp2(m_old - m_new)
            l_ref[j] = alpha * l_ref[j] + jnp.sum(p, axis=0, keepdims=True)
            acc_ref[j] = alpha * acc_ref[j] + _dot(v_blk, p.astype(BF16))
            m_ref[j] = m_new

    scores(0, 0)

    def body(i, carry):
        b0 = 2 * i
        scores(b0 + 1, 1)
        softmax_pv(b0, 0, None)
        scores(b0 + 2, 0)
        softmax_pv(b0 + 1, 1, None)
        return carry

    lax.fori_loop(0, qi // 2, body, 0)
    kk = lax.broadcasted_iota(jnp.int32, (tk, tq), 0)
    qq = lax.broadcasted_iota(jnp.int32, (tk, tq), 1)
    causal = kk <= qq

    @pl.when(qi % 2 == 0)
    def _():
        softmax_pv(qi, 0, causal)

    @pl.when(qi % 2 == 1)
    def _():
        scores(qi, 1)
        softmax_pv(qi - 1, 0, None)
        softmax_pv(qi, 1, causal)

    lv = lam_ref[...]
    lam = (jnp.exp(jnp.sum(lv[0:1] * lv[1:2], axis=-1, keepdims=True))
           - jnp.exp(jnp.sum(lv[2:3] * lv[3:4], axis=-1, keepdims=True)) + lambda_init)
    o = acc_ref[0] / l_ref[0] - lam * (acc_ref[1] / l_ref[1])
    g_col = jnp.concatenate([g_ref[...]] * (tq // LANES), axis=1)
    o = o * lax.rsqrt(jnp.mean(o * o, axis=0, keepdims=True) + RMS_EPS) * g_col
    o_ref[...] = (o * (1.0 - lambda_init)).T.astype(BF16)


def _attention(qt, k, vt, pos, slopes, lam_vecs, g_wide, lambda_init, batch, seq):
    t = k.shape[0]
    tq = qt.shape[2]
    nq = seq // tq
    return pl.pallas_call(
        functools.partial(_attn_kernel, lambda_init),
        grid=(batch, ATTN_HEADS, nq),
        in_specs=[
            pl.BlockSpec((1, ATTN_V_DIM, tq), lambda b, h, q: (b * nq + q, h, 0)),
            pl.BlockSpec((seq, ATTN_V_DIM), lambda b, h, q: (b, h)),
            pl.BlockSpec((nq, ATTN_V_DIM, tq), lambda b, h, q: (b, h, 0)),
            _resident((tq, LANES)),
            pl.BlockSpec((1, SUBLANES, LANES), lambda b, h, q: (h, 0, 0)),
            _resident((SUBLANES, LANES)),
            _resident((ATTN_V_DIM, LANES)),
        ],
        out_specs=pl.BlockSpec((tq, ATTN_V_DIM), lambda b, h, q: (b * nq + q, h)),
        out_shape=jax.ShapeDtypeStruct((t, D_MODEL), BF16),
        scratch_shapes=[
            pltpu.VMEM((2, 2, tq, tq), F32),
            pltpu.VMEM((2, 1, tq), F32),
            pltpu.VMEM((2, 1, tq), F32),
            pltpu.VMEM((2, ATTN_V_DIM, tq), F32),
        ],
        compiler_params=_params(3),
        name="diff_attention",
    )(qt, k, vt, pos, slopes, lam_vecs, g_wide)


def _ffn_kernel(tiles_per_seq, fc, y_ref, wo_ref, gmix_ref, x_ref, gpre_ref, wup_ref, cw_ref, cb_ref,
                wdn_ref, gpost_ref, o_ref, act_ref, carry_ref):
    @pl.when(pl.program_id(0) % tiles_per_seq == 0)
    def _():
        carry_ref[...] = jnp.zeros_like(carry_ref)

    x = x_ref[...] + _rms(_dot(y_ref[...], wo_ref[...]), gmix_ref[...])
    hn = _rms(x, gpre_ref[...]).astype(BF16)
    for c in range(D_FF // fc):
        halves = []
        for part in range(2):
            cols = slice(part * D_FF + c * fc, part * D_FF + (c + 1) * fc)
            h = _dot(hn, wup_ref[:, cols])
            halves.append(_causal_conv(h, carry_ref, cw_ref, cb_ref, cols, FFN_CONV))
        act_ref[:, c * fc:(c + 1) * fc] = (_silu(halves[0]) * halves[1]).astype(BF16)
    f = _dot(act_ref[...], wdn_ref[...])
    o_ref[...] = x + _rms(f, gpost_ref[...])


def _mixer_out_ffn(y, wo, gmix, x, gpre, wup, conv_w, conv_b, wdn, gpost, seq):
    t, k = y.shape
    tm = min(ROW_TILE, seq)
    fc = FFN_CHUNK
    row = lambda i: (i, 0)
    return pl.pallas_call(
        functools.partial(_ffn_kernel, seq // tm, fc),
        grid=(t // tm,),
        in_specs=[
            pl.BlockSpec((tm, k), row),
            _resident((k, D_MODEL)),
            _resident((1, D_MODEL)),
            pl.BlockSpec((tm, D_MODEL), row),
            _resident((1, D_MODEL)),
            _resident((D_MODEL, 2 * D_FF)),
            _resident((FFN_CONV, 2 * D_FF)),
            _resident((1, 2 * D_FF)),
            _resident((D_FF, D_MODEL)),
            _resident((1, D_MODEL)),
        ],
        out_specs=pl.BlockSpec((tm, D_MODEL), row),
        out_shape=jax.ShapeDtypeStruct((t, D_MODEL), F32),
        scratch_shapes=[
            pltpu.VMEM((tm, D_FF), BF16),
            pltpu.VMEM((SUBLANES, 2 * D_FF), F32),
        ],
        compiler_params=_params(1),
        name="mixer_out_ffn",
    )(y, wo, gmix, x, gpre, wup, conv_w, conv_b, wdn, gpost)


def _row(v):
    return v.astype(F32).reshape(1, -1)


def _pad_lanes(v):
    v = v.astype(F32).reshape(1, -1)
    return jnp.pad(v, ((0, 0), (0, LANES - v.shape[1])))


def _head_expand_matrix():
    head_of_lane = jnp.arange(D_INNER) // SSM_HEAD_DIM
    return (jnp.arange(LANES)[:, None] == head_of_lane[None, :]).astype(BF16)


def _alibi_features(tk):
    kk = jnp.arange(tk)
    base = int(POS_DIGIT_BASE)
    cols = jnp.stack([kk // base, kk % base, kk // base, kk % base], axis=1).astype(F32)
    return jnp.pad(cols, ((0, 0), (0, LANES - 4))).astype(BF16)


def kernel(x, ssm_w_in, ssm_conv_w, ssm_conv_b, ssm_dt_bias, ssm_A_log, ssm_D, ssm_norm_g, ssm_w_out, attn_w_qkv, attn_lambda_q1, attn_lambda_k1, attn_lambda_q2, attn_lambda_k2, attn_subln_g, attn_w_out, norm_mix_pre, norm_mix_post, norm_ffn_pre, norm_ffn_post, ffn_w_up, ffn_conv_w, ffn_conv_b, ffn_w_down):
    batch, seq, d_model = x.shape
    depth = norm_mix_pre.shape[0]
    assert d_model == D_MODEL and seq % ROW_TILE == 0 and seq % ATTN_TILE == 0
    t = batch * seq
    xs = x.reshape(t, D_MODEL).astype(F32)

    expand = _head_expand_matrix()
    tq = min(ATTN_TILE, seq)
    pos = _alibi_features(tq)
    slopes = jnp.exp2(-8.0 * jnp.arange(1, ATTN_HEADS + 1, dtype=F32) / ATTN_HEADS)
    slopes = jnp.broadcast_to(slopes[:, None, None], (ATTN_HEADS, SUBLANES, LANES))

    for i in range(depth):
        j = i // 2
        if i % 2 == 0:
            w_in = ssm_w_in[j]
            wz = w_in[:, :D_INNER].astype(BF16)
            wx = w_in[:, D_INNER:D_INNER + SSM_CONV_DIM].astype(BF16)
            wdt = jnp.pad(w_in[:, D_INNER + SSM_CONV_DIM:],
                          ((0, 0), (0, LANES - SSM_HEADS))).astype(BF16)
            z, xbc, dt = _ssm_inproj(xs, _row(norm_mix_pre[i]), wz, wx, wdt,
                                     ssm_conv_w[j].astype(F32), _row(ssm_conv_b[j]), seq)
            y = _ssd(xbc, z, dt, _pad_lanes(ssm_dt_bias[j]), _pad_lanes(ssm_A_log[j]),
                     _row(jnp.repeat(ssm_D[j], SSM_HEAD_DIM)), _row(ssm_norm_g[j]), expand,
                     batch, seq)
            mixed, w_o = y, ssm_w_out[j]
        else:
            lambda_init = 0.8 - 0.6 * math.exp(-0.3 * i)
            w_qkv = attn_w_qkv[j]
            wqt = w_qkv[:, :D_MODEL].T.astype(BF16)
            wk = w_qkv[:, D_MODEL:2 * D_MODEL].astype(BF16)
            wvt = w_qkv[:, 2 * D_MODEL:].T.astype(BF16)
            qt, k, vt = _qkv(xs, _row(norm_mix_pre[i]), wqt, wk, wvt, seq)
            lam_vecs = jnp.stack([attn_lambda_q1[j], attn_lambda_k1[j],
                                  attn_lambda_q2[j], attn_lambda_k2[j]]).astype(F32)
            lam_vecs = jnp.pad(lam_vecs, ((0, SUBLANES - 4), (0, LANES - ATTN_HEAD_DIM)))
            g_wide = jnp.broadcast_to(attn_subln_g[j].astype(F32)[:, None], (ATTN_V_DIM, LANES))
            a = _attention(qt, k, vt, pos, slopes, lam_vecs, g_wide, lambda_init, batch, seq)
            mixed, w_o = a, attn_w_out[j]
        xs = _mixer_out_ffn(mixed, w_o.astype(BF16), _row(norm_mix_post[i]), xs,
                            _row(norm_ffn_pre[i]), ffn_w_up[i].astype(BF16),
                            ffn_conv_w[i].astype(F32), _row(ffn_conv_b[i]),
                            ffn_w_down[i].astype(BF16), _row(norm_ffn_post[i]), seq)
    return xs.reshape(batch, seq, D_MODEL).astype(x.dtype)
```

```python
import functools
import math

import jax
import jax.numpy as jnp
from jax import lax
from jax.experimental import pallas as pl
from jax.experimental.pallas import tpu as pltpu

F32 = jnp.float32
BF16 = jnp.bfloat16

D_MODEL = 1024
D_INNER = 2048
SSM_HEAD_DIM = 64
SSM_HEADS = 32
SSM_GROUPS = 8
SSM_STATE = 128
SSM_CONV = 4
SSM_CHUNK = 128
SSM_BC_DIM = SSM_GROUPS * SSM_STATE
SSM_CONV_DIM = D_INNER + 2 * SSM_BC_DIM
SSM_IN_DIM = D_INNER + SSM_CONV_DIM + SSM_HEADS
HEADS_PER_GROUP = SSM_HEADS // SSM_GROUPS
GROUP_WIDTH = D_INNER // SSM_GROUPS
ATTN_HEAD_DIM = 64
ATTN_HEADS = 8
ATTN_V_DIM = 128
D_FF = 2816
FFN_CONV = 3
RMS_EPS = 1e-6
LOG2_E = math.log2(math.e)
POS_DIGIT_BASE = 16.0

LANES = 128
SUBLANES = 8
V7X_VMEM_LIMIT_BYTES = 56 * 1024 * 1024

ROW_TILE = 512
ATTN_TILE = 512
N_CHUNK = 512
FFN_CHUNK = 256
SSD_CHUNKS_PER_STEP = 4


def _params(n_axes):
    return pltpu.CompilerParams(
        dimension_semantics=("arbitrary",) * n_axes,
        vmem_limit_bytes=V7X_VMEM_LIMIT_BYTES,
    )


def _resident(shape):
    zeros = (0,) * len(shape)
    return pl.BlockSpec(shape, lambda *_: zeros, pipeline_mode=pl.Buffered(1))


def _layer(index, shape):
    idx = (index,) + (0,) * len(shape)
    return pl.BlockSpec((None,) + tuple(shape), lambda *_: idx, pipeline_mode=pl.Buffered(1))


def _rms(x, g):
    return x * lax.rsqrt(jnp.mean(x * x, axis=-1, keepdims=True) + RMS_EPS) * g


def _silu(x):
    half = 0.5 * x
    return half + half * jnp.tanh(half)


def _dot(a, b):
    return jnp.dot(a, b, preferred_element_type=F32)


def _dot_nt(a, b):
    return lax.dot_general(a, b, (((1,), (1,)), ((), ())), preferred_element_type=F32)


def _causal_conv(h, carry_ref, w_ref, b_ref, cols, width, horner):
    rows, c = h.shape
    n = rows // SUBLANES + 1
    prev = carry_ref[:, cols]
    carry_ref[:, cols] = h[rows - SUBLANES:rows, :]
    tiles = jnp.concatenate([prev, h], axis=0).reshape(n, SUBLANES, c)
    sub = lax.broadcasted_iota(jnp.int32, (n, SUBLANES, c), 1)

    def shift(a, k):
        rot = pltpu.roll(a, k, 1)
        above = jnp.concatenate([rot[:1], rot[:-1]], axis=0)
        return jnp.where(sub < k, above, rot)

    def tap(k):
        return tiles * w_ref[k:k + 1, cols].reshape(1, 1, c)

    if horner:
        a = tap(0)
        for k in range(1, width):
            a = shift(a, 1) + tap(k)
    else:
        a = tap(width - 1)
        for k in range(1, width):
            a = a + shift(tiles, k) * w_ref[width - 1 - k:width - k, cols].reshape(1, 1, c)
    return a[1:].reshape(rows, c) + b_ref[:, cols]


def _ssm_inproj_kernel(tiles_per_seq, x_ref, g_ref, w_ref, cw_ref, cb_ref,
                       z_ref, xbc_ref, dt_ref, carry_ref):
    @pl.when(pl.program_id(0) % tiles_per_seq == 0)
    def _():
        carry_ref[...] = jnp.zeros_like(carry_ref)

    hn = _rms(x_ref[...], g_ref[...]).astype(BF16)
    dt_ref[...] = _dot(hn, w_ref[:, D_INNER + SSM_CONV_DIM:])
    n_chunks = SSM_CONV_DIM // N_CHUNK
    zc = D_INNER // n_chunks
    for c in range(n_chunks):
        cols = slice(c * N_CHUNK, (c + 1) * N_CHUNK)
        h = _dot(hn, w_ref[:, D_INNER + c * N_CHUNK:D_INNER + (c + 1) * N_CHUNK])
        zcols = slice(c * zc, (c + 1) * zc)
        z_ref[:, zcols] = _dot(hn, w_ref[:, zcols]).astype(BF16)
        y = _causal_conv(h, carry_ref, cw_ref, cb_ref, cols, SSM_CONV, horner=True)
        xbc_ref[:, cols] = _silu(y).astype(BF16)


def _ssm_inproj(x, g, w_in, conv_w, conv_b, layer, ssm_layer, seq):
    t = x.shape[0]
    tm = min(ROW_TILE, seq)
    row = lambda i: (i, 0)
    return pl.pallas_call(
        functools.partial(_ssm_inproj_kernel, seq // tm),
        grid=(t // tm,),
        in_specs=[
            pl.BlockSpec((tm, D_MODEL), row),
            _layer(layer, (1, D_MODEL)),
            _layer(ssm_layer, (D_MODEL, SSM_IN_DIM)),
            _layer(ssm_layer, (SSM_CONV, SSM_CONV_DIM)),
            _layer(ssm_layer, (1, SSM_CONV_DIM)),
        ],
        out_specs=[
            pl.BlockSpec((tm, D_INNER), row),
            pl.BlockSpec((tm, SSM_CONV_DIM), row),
            pl.BlockSpec((tm, SSM_HEADS), row),
        ],
        out_shape=[
            jax.ShapeDtypeStruct((t, D_INNER), BF16),
            jax.ShapeDtypeStruct((t, SSM_CONV_DIM), BF16),
            jax.ShapeDtypeStruct((t, SSM_HEADS), F32),
        ],
        scratch_shapes=[pltpu.VMEM((SUBLANES, SSM_CONV_DIM), F32)],
        compiler_params=_params(1),
        name="ssm_inproj",
    )(x, g, w_in, conv_w, conv_b)


def _ssd_chunk(rows, xbc_ref, z_ref, dt_ref, dtb_ref, alog_ref, dexp_ref, ng_ref, e_ref,
               y_ref, state_ref):
    L = SSM_CHUNK
    H = SSM_HEADS

    dt_raw = dt_ref[rows, :] + dtb_ref[...]
    dt = jnp.maximum(dt_raw, 0.0) + jnp.log(1.0 + jnp.exp(-jnp.abs(dt_raw)))
    acs = dt * (-jnp.exp(alog_ref[...]) * LOG2_E)
    row_h = lax.broadcasted_iota(jnp.int32, (L, H), 0)
    k = 1
    while k < L:
        acs = acs + jnp.where(row_h >= k, pltpu.roll(acs, k, 0), 0.0)
        k *= 2
    acs_t = acs.T
    rowp_t = acs_t - jnp.log2(dt.T)
    w_t = jnp.exp2(acs_t[:, L - 1:L] - rowp_t)

    cd = jnp.broadcast_to(jnp.exp2(acs[L - 1:L, :]), (SUBLANES, H))
    cd_hi = cd.astype(BF16)
    cd_lo = (cd - cd_hi.astype(F32)).astype(BF16)
    cd_wide = (_dot(cd_hi, e_ref[...]) + _dot(cd_lo, e_ref[...]))[0:1, :]

    row = lax.broadcasted_iota(jnp.int32, (L, LANES), 0)
    lane = lax.broadcasted_iota(jnp.int32, (L, LANES), 1)
    causal = row >= lane
    keep_l = jnp.where(lane < SSM_HEAD_DIM, 1.0, 0.0).astype(BF16)
    keep_r = jnp.where(lane < SSM_HEAD_DIM, 0.0, 1.0).astype(BF16)

    for g in range(SSM_GROUPS):
        b_g = xbc_ref[rows, D_INNER + g * SSM_STATE:D_INNER + (g + 1) * SSM_STATE]
        c_g = xbc_ref[rows, D_INNER + SSM_BC_DIM + g * SSM_STATE:
                      D_INNER + SSM_BC_DIM + (g + 1) * SSM_STATE]
        cb = _dot_nt(c_g, b_g)
        b_t = b_g.astype(F32).T
        c_f = c_g.astype(F32)
        ys = []
        for pr in range(HEADS_PER_GROUP // 2):
            heads = (g * HEADS_PER_GROUP + 2 * pr, g * HEADS_PER_GROUP + 2 * pr + 1)
            cols = slice((g * 2 + pr) * LANES, (g * 2 + pr + 1) * LANES)
            scols = slice(pr * LANES, (pr + 1) * LANES)
            xp = xbc_ref[rows, cols]
            sp = state_ref[g, :, scols]
            m_parts, c_parts, bw_parts = [], [], []
            for h in heads:
                a_col = jnp.broadcast_to(acs[:, h:h + 1], (L, L))
                a_row = jnp.broadcast_to(rowp_t[h:h + 1, :], (L, L))
                m_parts.append(
                    (cb * jnp.exp2(jnp.where(causal, a_col - a_row, -jnp.inf))).astype(BF16))
                c_parts.append((c_f * jnp.exp2(a_col)).astype(BF16))
                bw_parts.append((b_t * jnp.broadcast_to(w_t[h:h + 1, :], (L, L))).astype(BF16))
            sb = sp.astype(BF16)
            x_blk = jnp.concatenate([xp * keep_l, xp * keep_r], axis=0)
            y_p = _dot(jnp.concatenate(m_parts + c_parts, axis=1),
                       jnp.concatenate([x_blk, sb * keep_l, sb * keep_r], axis=0))
            y_p = y_p + xp.astype(F32) * dexp_ref[:, cols]
            state_ref[g, :, scols] = sp * cd_wide[:, cols] + _dot(
                jnp.concatenate(bw_parts, axis=1), x_blk)
            ys.append(y_p)
        gcols = slice(g * GROUP_WIDTH, (g + 1) * GROUP_WIDTH)
        y_g = jnp.concatenate(ys, axis=1) * _silu(z_ref[rows, gcols].astype(F32))
        y_ref[rows, gcols] = _rms(y_g, ng_ref[:, gcols]).astype(BF16)


def _ssd_kernel(*refs):
    state_ref = refs[-1]

    @pl.when(pl.program_id(1) == 0)
    def _():
        state_ref[...] = jnp.zeros_like(state_ref)

    for sub in range(SSD_CHUNKS_PER_STEP):
        _ssd_chunk(slice(sub * SSM_CHUNK, (sub + 1) * SSM_CHUNK), *refs)


def _ssd(xbc, z, dt, dt_bias, a_log, d_wide, norm_g, expand, ssm_layer, batch, seq):
    t = xbc.shape[0]
    rows = SSD_CHUNKS_PER_STEP * SSM_CHUNK
    n_steps = seq // rows
    row = lambda b, c: (b * n_steps + c, 0)
    return pl.pallas_call(
        _ssd_kernel,
        grid=(batch, n_steps),
        in_specs=[
            pl.BlockSpec((rows, SSM_CONV_DIM), row),
            pl.BlockSpec((rows, D_INNER), row),
            pl.BlockSpec((rows, SSM_HEADS), row),
            _layer(ssm_layer, (1, SSM_HEADS)),
            _layer(ssm_layer, (1, SSM_HEADS)),
            _layer(ssm_layer, (1, D_INNER)),
            _layer(ssm_layer, (1, D_INNER)),
            _resident((SSM_HEADS, D_INNER)),
        ],
        out_specs=pl.BlockSpec((rows, D_INNER), row),
        out_shape=jax.ShapeDtypeStruct((t, D_INNER), BF16),
        scratch_shapes=[pltpu.VMEM((SSM_GROUPS, SSM_STATE, GROUP_WIDTH), F32)],
        compiler_params=_params(2),
        name="ssd_scan",
    )(xbc, z, dt, dt_bias, a_log, d_wide, norm_g, expand)


def _qkv_kernel(x_ref, g_ref, w_ref, qt_ref, k_ref, vt_ref, wqt_ref, wvt_ref):
    @pl.when(pl.program_id(0) == 0)
    def _():
        wqt_ref[...] = w_ref[:, 0:D_MODEL].T
        wvt_ref[...] = w_ref[:, 2 * D_MODEL:3 * D_MODEL].T

    hn = _rms(x_ref[...], g_ref[...]).astype(BF16)
    scale = ATTN_HEAD_DIM ** -0.5 * LOG2_E
    for c in range(D_MODEL // N_CHUNK):
        rows = slice(c * N_CHUNK, (c + 1) * N_CHUNK)
        qt_ref[0, rows, :] = (_dot_nt(wqt_ref[rows, :], hn) * scale).astype(BF16)
        vt_ref[0, rows, :] = _dot_nt(wvt_ref[rows, :], hn).astype(BF16)
        k_ref[:, rows] = _dot(hn, w_ref[:, D_MODEL + c * N_CHUNK:D_MODEL + (c + 1) * N_CHUNK]
                              ).astype(BF16)


def _qkv(x, g, w_qkv, layer, attn_layer, seq):
    t = x.shape[0]
    tm = min(ATTN_TILE, seq)
    row = lambda i: (i, 0)
    tile = lambda i: (i, 0, 0)
    return pl.pallas_call(
        _qkv_kernel,
        grid=(t // tm,),
        in_specs=[
            pl.BlockSpec((tm, D_MODEL), row),
            _layer(layer, (1, D_MODEL)),
            _layer(attn_layer, (D_MODEL, 3 * D_MODEL)),
        ],
        out_specs=[
            pl.BlockSpec((1, D_MODEL, tm), tile),
            pl.BlockSpec((tm, D_MODEL), row),
            pl.BlockSpec((1, D_MODEL, tm), tile),
        ],
        out_shape=[
            jax.ShapeDtypeStruct((t // tm, D_MODEL, tm), BF16),
            jax.ShapeDtypeStruct((t, D_MODEL), BF16),
            jax.ShapeDtypeStruct((t // tm, D_MODEL, tm), BF16),
        ],
        scratch_shapes=[
            pltpu.VMEM((D_MODEL, D_MODEL), BF16),
            pltpu.VMEM((D_MODEL, D_MODEL), BF16),
        ],
        compiler_params=_params(1),
        name="attn_qkv",
    )(x, g, w_qkv)


def _attn_kernel(lambda_init, qt_ref, k_ref, vt_ref, pos_ref, slope_ref, lam_ref, g_ref,
                 o_ref, s_ref, m_ref, l_ref, acc_ref):
    tq = qt_ref.shape[2]
    tk = tq
    qi = pl.program_id(2)

    slope = jnp.concatenate([slope_ref[0]] * (tq // LANES), axis=1) * LOG2_E
    s_hi = slope.astype(BF16).astype(F32)
    s_lo = slope - s_hi
    r8 = lax.broadcasted_iota(jnp.int32, (SUBLANES, tq), 0)
    feat = jnp.where(r8 == 0, POS_DIGIT_BASE * s_hi,
                     jnp.where(r8 == 1, s_hi,
                               jnp.where(r8 == 2, POS_DIGIT_BASE * s_lo,
                                         jnp.where(r8 == 3, s_lo, 0.0)))).astype(BF16)
    aug = jnp.concatenate([feat, jnp.zeros((LANES - SUBLANES, tq), BF16)], axis=0)
    zero_half = jnp.zeros((ATTN_HEAD_DIM, tq), BF16)
    q_all = qt_ref[0]
    q_aug = (jnp.concatenate([q_all[0:ATTN_HEAD_DIM], zero_half, aug], axis=0),
             jnp.concatenate([zero_half, q_all[ATTN_HEAD_DIM:], aug], axis=0))
    slope_row = slope[0:1, :]
    pos = pos_ref[...]

    m_ref[...] = jnp.full(m_ref.shape, -jnp.inf, F32)
    l_ref[...] = jnp.zeros_like(l_ref)
    acc_ref[...] = jnp.zeros_like(acc_ref)

    def scores(ki, slot):
        k_blk = k_ref[pl.ds(pl.multiple_of(ki * tk, tk), tk), :]
        k_aug = jnp.concatenate([k_blk, pos], axis=1)
        for j in range(2):
            s_ref[slot, j] = _dot(k_aug, q_aug[j])

    def softmax_pv(ki, slot, mask):
        v_blk = vt_ref[ki]
        c_blk = slope_row * ((ki - qi) * tk).astype(F32)
        for j in range(2):
            s = s_ref[slot, j]
            if mask is not None:
                s = jnp.where(mask, s, -jnp.inf)
            m_old = m_ref[j]
            m_new = jnp.maximum(m_old, jnp.max(s, axis=0, keepdims=True) + c_blk)
            p = jnp.exp2(s - (m_new - c_blk))
            alpha = jnp.exp2(m_old - m_new)
            l_ref[j] = alpha * l_ref[j] + jnp.sum(p, axis=0, keepdims=True)
            acc_ref[j] = alpha * acc_ref[j] + _dot(v_blk, p.astype(BF16))
            m_ref[j] = m_new

    scores(0, 0)

    def body(i, carry):
        b0 = 2 * i
        scores(b0 + 1, 1)
        softmax_pv(b0, 0, None)
        scores(b0 + 2, 0)
        softmax_pv(b0 + 1, 1, None)
        return carry

    lax.fori_loop(0, qi // 2, body, 0)
    kk = lax.broadcasted_iota(jnp.int32, (tk, tq), 0)
    qq = lax.broadcasted_iota(jnp.int32, (tk, tq), 1)
    causal = kk <= qq

    @pl.when(qi % 2 == 0)
    def _():
        softmax_pv(qi, 0, causal)

    @pl.when(qi % 2 == 1)
    def _():
        scores(qi, 1)
        softmax_pv(qi - 1, 0, None)
        softmax_pv(qi, 1, causal)

    lv = lam_ref[...]
    lam = (jnp.exp(jnp.sum(lv[0:1] * lv[1:2], axis=-1, keepdims=True))
           - jnp.exp(jnp.sum(lv[2:3] * lv[3:4], axis=-1, keepdims=True)) + lambda_init)
    o = acc_ref[0] / l_ref[0] - lam * (acc_ref[1] / l_ref[1])
    g_col = jnp.concatenate([g_ref[...]] * (tq // LANES), axis=1)
    o = o * lax.rsqrt(jnp.mean(o * o, axis=0, keepdims=True) + RMS_EPS) * g_col
    o_ref[...] = (o * (1.0 - lambda_init)).T.astype(BF16)


def _attention(qt, k, vt, pos, slopes, lam_vecs, g_wide, lambda_init, attn_layer, batch, seq):
    t = k.shape[0]
    tq = qt.shape[2]
    nq = seq // tq
    return pl.pallas_call(
        functools.partial(_attn_kernel, lambda_init),
        grid=(batch, ATTN_HEADS, nq),
        in_specs=[
            pl.BlockSpec((1, ATTN_V_DIM, tq), lambda b, h, q: (b * nq + q, h, 0)),
            pl.BlockSpec((seq, ATTN_V_DIM), lambda b, h, q: (b, h)),
            pl.BlockSpec((nq, ATTN_V_DIM, tq), lambda b, h, q: (b, h, 0)),
            _resident((tq, LANES)),
            pl.BlockSpec((1, SUBLANES, LANES), lambda b, h, q: (h, 0, 0)),
            _layer(attn_layer, (4, ATTN_HEAD_DIM)),
            _layer(attn_layer, (ATTN_V_DIM, LANES)),
        ],
        out_specs=pl.BlockSpec((tq, ATTN_V_DIM), lambda b, h, q: (b * nq + q, h)),
        out_shape=jax.ShapeDtypeStruct((t, D_MODEL), BF16),
        scratch_shapes=[
            pltpu.VMEM((2, 2, tq, tq), F32),
            pltpu.VMEM((2, 1, tq), F32),
            pltpu.VMEM((2, 1, tq), F32),
            pltpu.VMEM((2, ATTN_V_DIM, tq), F32),
        ],
        compiler_params=_params(3),
        name="diff_attention",
    )(qt, k, vt, pos, slopes, lam_vecs, g_wide)


def _ffn_kernel(tiles_per_seq, fc, y_ref, wo_ref, gmix_ref, x_ref, gpre_ref, wup_ref, cw_ref, cb_ref,
                wdn_ref, gpost_ref, o_ref, act_ref, carry_ref):
    @pl.when(pl.program_id(0) % tiles_per_seq == 0)
    def _():
        carry_ref[...] = jnp.zeros_like(carry_ref)

    x = x_ref[...] + _rms(_dot(y_ref[...], wo_ref[...]), gmix_ref[...])
    hn = _rms(x, gpre_ref[...]).astype(BF16)
    for c in range(D_FF // fc):
        halves = []
        for part in range(2):
            cols = slice(part * D_FF + c * fc, part * D_FF + (c + 1) * fc)
            h = _dot(hn, wup_ref[:, cols])
            halves.append(_causal_conv(h, carry_ref, cw_ref, cb_ref, cols, FFN_CONV, horner=False))
        act_ref[:, c * fc:(c + 1) * fc] = (_silu(halves[0]) * halves[1]).astype(BF16)
    f = _dot(act_ref[...], wdn_ref[...])
    o_ref[...] = x + _rms(f, gpost_ref[...])


def _mixer_out_ffn(y, wo, gmix, x, gpre, wup, conv_w, conv_b, wdn, gpost, layer, mix_layer, seq):
    t, k = y.shape
    tm = min(ROW_TILE, seq)
    fc = FFN_CHUNK
    row = lambda i: (i, 0)
    return pl.pallas_call(
        functools.partial(_ffn_kernel, seq // tm, fc),
        grid=(t // tm,),
        in_specs=[
            pl.BlockSpec((tm, k), row),
            _layer(mix_layer, (k, D_MODEL)),
            _layer(layer, (1, D_MODEL)),
            pl.BlockSpec((tm, D_MODEL), row),
            _layer(layer, (1, D_MODEL)),
            _layer(layer, (D_MODEL, 2 * D_FF)),
            _layer(layer, (FFN_CONV, 2 * D_FF)),
            _layer(layer, (1, 2 * D_FF)),
            _layer(layer, (D_FF, D_MODEL)),
            _layer(layer, (1, D_MODEL)),
        ],
        out_specs=pl.BlockSpec((tm, D_MODEL), row),
        out_shape=jax.ShapeDtypeStruct((t, D_MODEL), F32),
        scratch_shapes=[
            pltpu.VMEM((tm, D_FF), BF16),
            pltpu.VMEM((SUBLANES, 2 * D_FF), F32),
        ],
        compiler_params=_params(1),
        name="mixer_out_ffn",
    )(y, wo, gmix, x, gpre, wup, conv_w, conv_b, wdn, gpost)


def _rows(v):
    return v.astype(F32)[:, None, :]


def _head_expand_matrix():
    head_of_lane = jnp.arange(D_INNER) // SSM_HEAD_DIM
    return (jnp.arange(SSM_HEADS)[:, None] == head_of_lane[None, :]).astype(BF16)


def _alibi_features(tk):
    kk = jnp.arange(tk)
    base = int(POS_DIGIT_BASE)
    cols = jnp.stack([kk // base, kk % base, kk // base, kk % base], axis=1).astype(F32)
    return jnp.pad(cols, ((0, 0), (0, LANES - 4))).astype(BF16)


def kernel(x, ssm_w_in, ssm_conv_w, ssm_conv_b, ssm_dt_bias, ssm_A_log, ssm_D, ssm_norm_g, ssm_w_out, attn_w_qkv, attn_lambda_q1, attn_lambda_k1, attn_lambda_q2, attn_lambda_k2, attn_subln_g, attn_w_out, norm_mix_pre, norm_mix_post, norm_ffn_pre, norm_ffn_post, ffn_w_up, ffn_conv_w, ffn_conv_b, ffn_w_down):
    batch, seq, d_model = x.shape
    depth = norm_mix_pre.shape[0]
    assert d_model == D_MODEL and seq % ROW_TILE == 0 and seq % ATTN_TILE == 0
    t = batch * seq
    xs = x.reshape(t, D_MODEL).astype(F32)

    w_in = ssm_w_in.astype(BF16)
    conv_w, conv_b = ssm_conv_w.astype(F32), _rows(ssm_conv_b)
    dt_bias, a_log = _rows(ssm_dt_bias), _rows(ssm_A_log)
    d_wide = _rows(jnp.repeat(ssm_D, SSM_HEAD_DIM, axis=1))
    ssm_g = _rows(ssm_norm_g)
    w_out_ssm = ssm_w_out.astype(BF16)
    w_qkv = attn_w_qkv.astype(BF16)
    lam_vecs = jnp.stack([attn_lambda_q1, attn_lambda_k1, attn_lambda_q2, attn_lambda_k2],
                         axis=1).astype(F32)
    g_wide = jnp.broadcast_to(attn_subln_g.astype(F32)[:, :, None],
                              attn_subln_g.shape + (LANES,))
    w_out_attn = attn_w_out.astype(BF16)
    g_mix_pre, g_mix_post = _rows(norm_mix_pre), _rows(norm_mix_post)
    g_ffn_pre, g_ffn_post = _rows(norm_ffn_pre), _rows(norm_ffn_post)
    w_up, w_down = ffn_w_up.astype(BF16), ffn_w_down.astype(BF16)
    f_conv_w, f_conv_b = ffn_conv_w.astype(F32), _rows(ffn_conv_b)

    expand = _head_expand_matrix()
    pos = _alibi_features(min(ATTN_TILE, seq))
    slopes = jnp.exp2(-8.0 * jnp.arange(1, ATTN_HEADS + 1, dtype=F32) / ATTN_HEADS)
    slopes = jnp.broadcast_to(slopes[:, None, None], (ATTN_HEADS, SUBLANES, LANES))

    for i in range(depth):
        j = i // 2
        if i % 2 == 0:
            z, xbc, dt = _ssm_inproj(xs, g_mix_pre, w_in, conv_w, conv_b, i, j, seq)
            mixed = _ssd(xbc, z, dt, dt_bias, a_log, d_wide, ssm_g, expand, j, batch, seq)
            w_o = w_out_ssm
        else:
            lambda_init = 0.8 - 0.6 * math.exp(-0.3 * i)
            qt, k, vt = _qkv(xs, g_mix_pre, w_qkv, i, j, seq)
            mixed = _attention(qt, k, vt, pos, slopes, lam_vecs, g_wide, lambda_init, j, batch, seq)
            w_o = w_out_attn
        xs = _mixer_out_ffn(mixed, w_o, g_mix_post, xs, g_ffn_pre, w_up, f_conv_w, f_conv_b,
                            w_down, g_ffn_post, i, j, seq)
    return xs.reshape(batch, seq, D_MODEL).astype(x.dtype)
```

```python
import functools
import math

import jax
import jax.numpy as jnp
from jax import lax
from jax.experimental import pallas as pl
from jax.experimental.pallas import tpu as pltpu

F32 = jnp.float32
BF16 = jnp.bfloat16

D_MODEL = 1024
D_INNER = 2048
SSM_HEAD_DIM = 64
SSM_HEADS = 32
SSM_GROUPS = 8
SSM_STATE = 128
SSM_CONV = 4
SSM_CHUNK = 128
SSM_BC_DIM = SSM_GROUPS * SSM_STATE
SSM_CONV_DIM = D_INNER + 2 * SSM_BC_DIM
SSM_IN_DIM = D_INNER + SSM_CONV_DIM + SSM_HEADS
SSM_IN_PAD = D_INNER + SSM_CONV_DIM + 128
HEADS_PER_GROUP = SSM_HEADS // SSM_GROUPS
GROUP_WIDTH = D_INNER // SSM_GROUPS
ATTN_HEAD_DIM = 64
ATTN_HEADS = 8
ATTN_V_DIM = 128
D_FF = 2816
FFN_CONV = 3
RMS_EPS = 1e-6
LOG2_E = math.log2(math.e)
POS_DIGIT_BASE = 16.0

LANES = 128
SUBLANES = 8
V7X_VMEM_LIMIT_BYTES = 56 * 1024 * 1024

ROW_TILE = 512
ATTN_TILE = 512
N_CHUNK = 512
FFN_CHUNK = 256
SSD_CHUNKS_PER_STEP = 4
ATTN_HEADS_PER_STEP = 4


def _params(n_axes):
    return pltpu.CompilerParams(
        dimension_semantics=("arbitrary",) * n_axes,
        vmem_limit_bytes=V7X_VMEM_LIMIT_BYTES,
    )


def _resident(shape):
    zeros = (0,) * len(shape)
    return pl.BlockSpec(shape, lambda *_: zeros, pipeline_mode=pl.Buffered(1))


def _layer(index, shape):
    idx = (index,) + (0,) * len(shape)
    return pl.BlockSpec((None,) + tuple(shape), lambda *_: idx, pipeline_mode=pl.Buffered(1))


def _rms(x, g):
    return x * lax.rsqrt(jnp.mean(x * x, axis=-1, keepdims=True) + RMS_EPS) * g


def _silu(x):
    half = 0.5 * x
    return half + half * jnp.tanh(half)


def _dot(a, b):
    return jnp.dot(a, b, preferred_element_type=F32)


def _dot_nt(a, b):
    return lax.dot_general(a, b, (((1,), (1,)), ((), ())), preferred_element_type=F32)


def _causal_conv(h, carry_ref, w_ref, b_ref, cols, width, horner):
    rows, c = h.shape
    n = rows // SUBLANES + 1
    prev = carry_ref[:, cols]
    carry_ref[:, cols] = h[rows - SUBLANES:rows, :]
    tiles = jnp.concatenate([prev, h], axis=0).reshape(n, SUBLANES, c)
    sub = lax.broadcasted_iota(jnp.int32, (n, SUBLANES, c), 1)

    def shift(a, k):
        rot = pltpu.roll(a, k, 1)
        above = jnp.concatenate([rot[:1], rot[:-1]], axis=0)
        return jnp.where(sub < k, above, rot)

    def tap(k):
        return tiles * w_ref[k:k + 1, cols].reshape(1, 1, c)

    if horner:
        a = tap(0)
        for k in range(1, width):
            a = shift(a, 1) + tap(k)
    else:
        a = tap(width - 1)
        for k in range(1, width):
            a = a + shift(tiles, k) * w_ref[width - 1 - k:width - k, cols].reshape(1, 1, c)
    return a[1:].reshape(rows, c) + b_ref[:, cols]


def _ssm_inproj_kernel(tiles_per_seq, x_ref, g_ref, w_ref, cw_ref, cb_ref,
                       z_ref, xbc_ref, dt_ref, carry_ref):
    @pl.when(pl.program_id(0) % tiles_per_seq == 0)
    def _():
        carry_ref[...] = jnp.zeros_like(carry_ref)

    hn = _rms(x_ref[...], g_ref[...]).astype(BF16)
    dt_ref[...] = _dot(hn, w_ref[:, D_INNER + SSM_CONV_DIM:])
    n_chunks = SSM_CONV_DIM // N_CHUNK
    zc = D_INNER // n_chunks
    for c in range(n_chunks):
        cols = slice(c * N_CHUNK, (c + 1) * N_CHUNK)
        h = _dot(hn, w_ref[:, D_INNER + c * N_CHUNK:D_INNER + (c + 1) * N_CHUNK])
        zcols = slice(c * zc, (c + 1) * zc)
        z_ref[:, zcols] = _dot(hn, w_ref[:, zcols]).astype(BF16)
        y = _causal_conv(h, carry_ref, cw_ref, cb_ref, cols, SSM_CONV, horner=True)
        xbc_ref[:, cols] = _silu(y).astype(BF16)


def _ssm_inproj(x, g, w_in, conv_w, conv_b, layer, ssm_layer, seq):
    t = x.shape[0]
    tm = min(ROW_TILE, seq)
    row = lambda i: (i, 0)
    return pl.pallas_call(
        functools.partial(_ssm_inproj_kernel, seq // tm),
        grid=(t // tm,),
        in_specs=[
            pl.BlockSpec((tm, D_MODEL), row),
            _layer(layer, (1, D_MODEL)),
            _layer(ssm_layer, (D_MODEL, SSM_IN_PAD)),
            _layer(ssm_layer, (SSM_CONV, SSM_CONV_DIM)),
            _layer(ssm_layer, (1, SSM_CONV_DIM)),
        ],
        out_specs=[
            pl.BlockSpec((tm, D_INNER), row),
            pl.BlockSpec((tm, SSM_CONV_DIM), row),
            pl.BlockSpec((tm, LANES), row),
        ],
        out_shape=[
            jax.ShapeDtypeStruct((t, D_INNER), BF16),
            jax.ShapeDtypeStruct((t, SSM_CONV_DIM), BF16),
            jax.ShapeDtypeStruct((t, LANES), F32),
        ],
        scratch_shapes=[pltpu.VMEM((SUBLANES, SSM_CONV_DIM), F32)],
        compiler_params=_params(1),
        name="ssm_inproj",
    )(x, g, w_in, conv_w, conv_b)


def _ssd_chunk(rows, xbc_ref, z_ref, dt_ref, dtb_ref, alog_ref, dexp_ref, ng_ref, e_ref,
               y_ref, state_ref):
    L = SSM_CHUNK
    H = SSM_HEADS

    dt_raw = dt_ref[rows, 0:H] + dtb_ref[...]
    dt = jnp.maximum(dt_raw, 0.0) + jnp.log(1.0 + jnp.exp(-jnp.abs(dt_raw)))
    acs = dt * (-jnp.exp(alog_ref[...]) * LOG2_E)
    row_h = lax.broadcasted_iota(jnp.int32, (L, H), 0)
    k = 1
    while k < L:
        acs = acs + jnp.where(row_h >= k, pltpu.roll(acs, k, 0), 0.0)
        k *= 2
    acs_t = acs.T
    rowp_t = acs_t - jnp.log2(dt.T)
    w_t = jnp.exp2(acs_t[:, L - 1:L] - rowp_t)

    cd = jnp.broadcast_to(jnp.exp2(acs[L - 1:L, :]), (SUBLANES, H))
    cd_hi = cd.astype(BF16)
    cd_lo = (cd - cd_hi.astype(F32)).astype(BF16)
    cd_wide = (_dot(cd_hi, e_ref[...]) + _dot(cd_lo, e_ref[...]))[0:1, :]

    row = lax.broadcasted_iota(jnp.int32, (L, LANES), 0)
    lane = lax.broadcasted_iota(jnp.int32, (L, LANES), 1)
    causal = row >= lane
    keep_l = jnp.where(lane < SSM_HEAD_DIM, 1.0, 0.0).astype(BF16)
    keep_r = jnp.where(lane < SSM_HEAD_DIM, 0.0, 1.0).astype(BF16)

    for g in range(SSM_GROUPS):
        b_g = xbc_ref[rows, D_INNER + g * SSM_STATE:D_INNER + (g + 1) * SSM_STATE]
        c_g = xbc_ref[rows, D_INNER + SSM_BC_DIM + g * SSM_STATE:
                      D_INNER + SSM_BC_DIM + (g + 1) * SSM_STATE]
        cb = _dot_nt(c_g, b_g)
        b_t = b_g.astype(F32).T
        c_f = c_g.astype(F32)
        ys = []
        for pr in range(HEADS_PER_GROUP // 2):
            heads = (g * HEADS_PER_GROUP + 2 * pr, g * HEADS_PER_GROUP + 2 * pr + 1)
            cols = slice((g * 2 + pr) * LANES, (g * 2 + pr + 1) * LANES)
            scols = slice(pr * LANES, (pr + 1) * LANES)
            xp = xbc_ref[rows, cols]
            sp = state_ref[g, :, scols]
            m_parts, c_parts, bw_parts = [], [], []
            for h in heads:
                a_col = jnp.broadcast_to(acs[:, h:h + 1], (L, L))
                a_row = jnp.broadcast_to(rowp_t[h:h + 1, :], (L, L))
                m_parts.append(
                    (cb * jnp.exp2(jnp.where(causal, a_col - a_row, -jnp.inf))).astype(BF16))
                c_parts.append((c_f * jnp.exp2(a_col)).astype(BF16))
                bw_parts.append((b_t * jnp.broadcast_to(w_t[h:h + 1, :], (L, L))).astype(BF16))
            sb = sp.astype(BF16)
            x_blk = jnp.concatenate([xp * keep_l, xp * keep_r], axis=0)
            y_p = _dot(jnp.concatenate(m_parts + c_parts, axis=1),
                       jnp.concatenate([x_blk, sb * keep_l, sb * keep_r], axis=0))
            y_p = y_p + xp.astype(F32) * dexp_ref[:, cols]
            state_ref[g, :, scols] = sp * cd_wide[:, cols] + _dot(
                jnp.concatenate(bw_parts, axis=1), x_blk)
            ys.append(y_p)
        gcols = slice(g * GROUP_WIDTH, (g + 1) * GROUP_WIDTH)
        y_g = jnp.concatenate(ys, axis=1) * _silu(z_ref[rows, gcols].astype(F32))
        y_ref[rows, gcols] = _rms(y_g, ng_ref[:, gcols]).astype(BF16)


def _ssd_kernel(*refs):
    state_ref = refs[-1]

    @pl.when(pl.program_id(1) == 0)
    def _():
        state_ref[...] = jnp.zeros_like(state_ref)

    for sub in range(SSD_CHUNKS_PER_STEP):
        _ssd_chunk(slice(sub * SSM_CHUNK, (sub + 1) * SSM_CHUNK), *refs)


def _ssd(xbc, z, dt, dt_bias, a_log, d_wide, norm_g, expand, ssm_layer, batch, seq):
    t = xbc.shape[0]
    rows = SSD_CHUNKS_PER_STEP * SSM_CHUNK
    n_steps = seq // rows
    row = lambda b, c: (b * n_steps + c, 0)
    return pl.pallas_call(
        _ssd_kernel,
        grid=(batch, n_steps),
        in_specs=[
            pl.BlockSpec((rows, SSM_CONV_DIM), row),
            pl.BlockSpec((rows, D_INNER), row),
            pl.BlockSpec((rows, LANES), row),
            _layer(ssm_layer, (1, SSM_HEADS)),
            _layer(ssm_layer, (1, SSM_HEADS)),
            _layer(ssm_layer, (1, D_INNER)),
            _layer(ssm_layer, (1, D_INNER)),
            _resident((SSM_HEADS, D_INNER)),
        ],
        out_specs=pl.BlockSpec((rows, D_INNER), row),
        out_shape=jax.ShapeDtypeStruct((t, D_INNER), BF16),
        scratch_shapes=[pltpu.VMEM((SSM_GROUPS, SSM_STATE, GROUP_WIDTH), F32)],
        compiler_params=_params(2),
        name="ssd_scan",
    )(xbc, z, dt, dt_bias, a_log, d_wide, norm_g, expand)


def _qkv_kernel(x_ref, g_ref, w_ref, qt_ref, k_ref, vt_ref, wqt_ref, wvt_ref):
    @pl.when(pl.program_id(0) == 0)
    def _():
        wqt_ref[...] = w_ref[:, 0:D_MODEL].T
        wvt_ref[...] = w_ref[:, 2 * D_MODEL:3 * D_MODEL].T

    hn = _rms(x_ref[...], g_ref[...]).astype(BF16)
    scale = ATTN_HEAD_DIM ** -0.5 * LOG2_E
    for c in range(D_MODEL // N_CHUNK):
        rows = slice(c * N_CHUNK, (c + 1) * N_CHUNK)
        qt_ref[0, rows, :] = (_dot_nt(wqt_ref[rows, :], hn) * scale).astype(BF16)
        vt_ref[0, rows, :] = _dot_nt(wvt_ref[rows, :], hn).astype(BF16)
        k_ref[:, rows] = _dot(hn, w_ref[:, D_MODEL + c * N_CHUNK:D_MODEL + (c + 1) * N_CHUNK]
                              ).astype(BF16)


def _qkv(x, g, w_qkv, layer, attn_layer, seq):
    t = x.shape[0]
    tm = min(ATTN_TILE, seq)
    row = lambda i: (i, 0)
    tile = lambda i: (i, 0, 0)
    return pl.pallas_call(
        _qkv_kernel,
        grid=(t // tm,),
        in_specs=[
            pl.BlockSpec((tm, D_MODEL), row),
            _layer(layer, (1, D_MODEL)),
            _layer(attn_layer, (D_MODEL, 3 * D_MODEL)),
        ],
        out_specs=[
            pl.BlockSpec((1, D_MODEL, tm), tile),
            pl.BlockSpec((tm, D_MODEL), row),
            pl.BlockSpec((1, D_MODEL, tm), tile),
        ],
        out_shape=[
            jax.ShapeDtypeStruct((t // tm, D_MODEL, tm), BF16),
            jax.ShapeDtypeStruct((t, D_MODEL), BF16),
            jax.ShapeDtypeStruct((t // tm, D_MODEL, tm), BF16),
        ],
        scratch_shapes=[
            pltpu.VMEM((D_MODEL, D_MODEL), BF16),
            pltpu.VMEM((D_MODEL, D_MODEL), BF16),
        ],
        compiler_params=_params(1),
        name="attn_qkv",
    )(x, g, w_qkv)


def _attn_kernel(lambda_init, qt_ref, k_ref, vt_ref, pos_ref, slope_ref, lam_ref, g_ref,
                 o_ref, s_ref, m_ref, l_ref, acc_ref):
    tq = qt_ref.shape[2]
    tk = tq
    qi = pl.program_id(2)
    heads = range(ATTN_HEADS_PER_STEP)

    r8 = lax.broadcasted_iota(jnp.int32, (SUBLANES, tq), 0)
    zero_half = jnp.zeros((ATTN_HEAD_DIM, tq), BF16)
    zero_pad = jnp.zeros((LANES - SUBLANES, tq), BF16)
    q_aug, slope_row = [], []
    for hh in heads:
        slope = jnp.concatenate([slope_ref[hh]] * (tq // LANES), axis=1) * LOG2_E
        s_hi = slope.astype(BF16).astype(F32)
        s_lo = slope - s_hi
        feat = jnp.where(r8 == 0, POS_DIGIT_BASE * s_hi,
                         jnp.where(r8 == 1, s_hi,
                                   jnp.where(r8 == 2, POS_DIGIT_BASE * s_lo,
                                             jnp.where(r8 == 3, s_lo, 0.0)))).astype(BF16)
        aug = jnp.concatenate([feat, zero_pad], axis=0)
        q_all = qt_ref[0, hh * ATTN_V_DIM:(hh + 1) * ATTN_V_DIM, :]
        q_aug.append((jnp.concatenate([q_all[0:ATTN_HEAD_DIM], zero_half, aug], axis=0),
                      jnp.concatenate([zero_half, q_all[ATTN_HEAD_DIM:], aug], axis=0)))
        slope_row.append(slope[0:1, :])
    pos = pos_ref[...]

    m_ref[...] = jnp.full(m_ref.shape, -jnp.inf, F32)
    l_ref[...] = jnp.zeros_like(l_ref)
    acc_ref[...] = jnp.zeros_like(acc_ref)

    def scores(ki, slot):
        rows = pl.ds(pl.multiple_of(ki * tk, tk), tk)
        for hh in heads:
            k_aug = jnp.concatenate(
                [k_ref[rows, hh * ATTN_V_DIM:(hh + 1) * ATTN_V_DIM], pos], axis=1)
            for j in range(2):
                s_ref[slot, hh, j] = _dot(k_aug, q_aug[hh][j])

    def softmax_pv(ki, slot, mask):
        for hh in heads:
            v_blk = vt_ref[ki, hh * ATTN_V_DIM:(hh + 1) * ATTN_V_DIM, :]
            c_blk = slope_row[hh] * ((ki - qi) * tk).astype(F32)
            for j in range(2):
                s = s_ref[slot, hh, j]
                if mask is not None:
                    s = jnp.where(mask, s, -jnp.inf)
                m_old = m_ref[hh, j]
                m_new = jnp.maximum(m_old, jnp.max(s, axis=0, keepdims=True) + c_blk)
                p = jnp.exp2(s - (m_new - c_blk))
                alpha = jnp.exp2(m_old - m_new)
                l_ref[hh, j] = alpha * l_ref[hh, j] + jnp.sum(p, axis=0, keepdims=True)
                acc_ref[hh, j] = alpha * acc_ref[hh, j] + _dot(v_blk, p.astype(BF16))
                m_ref[hh, j] = m_new

    scores(0, 0)

    def body(i, carry):
        b0 = 2 * i
        scores(b0 + 1, 1)
        softmax_pv(b0, 0, None)
        scores(b0 + 2, 0)
        softmax_pv(b0 + 1, 1, None)
        return carry

    lax.fori_loop(0, qi // 2, body, 0)
    kk = lax.broadcasted_iota(jnp.int32, (tk, tq), 0)
    qq = lax.broadcasted_iota(jnp.int32, (tk, tq), 1)
    causal = kk <= qq

    @pl.when(qi % 2 == 0)
    def _():
        softmax_pv(qi, 0, causal)

    @pl.when(qi % 2 == 1)
    def _():
        scores(qi, 1)
        softmax_pv(qi - 1, 0, None)
        softmax_pv(qi, 1, causal)

    lv = lam_ref[...]
    lam = (jnp.exp(jnp.sum(lv[0:1] * lv[1:2], axis=-1, keepdims=True))
           - jnp.exp(jnp.sum(lv[2:3] * lv[3:4], axis=-1, keepdims=True)) + lambda_init)
    g_col = jnp.concatenate([g_ref[...]] * (tq // LANES), axis=1)
    for hh in heads:
        o = acc_ref[hh, 0] / l_ref[hh, 0] - lam * (acc_ref[hh, 1] / l_ref[hh, 1])
        o = o * lax.rsqrt(jnp.mean(o * o, axis=0, keepdims=True) + RMS_EPS) * g_col
        o_ref[:, hh * ATTN_V_DIM:(hh + 1) * ATTN_V_DIM] = (
            o * (1.0 - lambda_init)).T.astype(BF16)


def _attention(qt, k, vt, pos, slopes, lam_vecs, g_wide, lambda_init, attn_layer, batch, seq):
    t = k.shape[0]
    tq = qt.shape[2]
    nq = seq // tq
    hps = ATTN_HEADS_PER_STEP
    width = hps * ATTN_V_DIM
    return pl.pallas_call(
        functools.partial(_attn_kernel, lambda_init),
        grid=(batch, ATTN_HEADS // hps, nq),
        in_specs=[
            pl.BlockSpec((1, width, tq), lambda b, h, q: (b * nq + q, h, 0)),
            pl.BlockSpec((seq, width), lambda b, h, q: (b, h)),
            pl.BlockSpec((nq, width, tq), lambda b, h, q: (b, h, 0)),
            _resident((tq, LANES)),
            pl.BlockSpec((hps, SUBLANES, LANES), lambda b, h, q: (h, 0, 0)),
            _layer(attn_layer, (4, ATTN_HEAD_DIM)),
            _layer(attn_layer, (ATTN_V_DIM, LANES)),
        ],
        out_specs=pl.BlockSpec((tq, width), lambda b, h, q: (b * nq + q, h)),
        out_shape=jax.ShapeDtypeStruct((t, D_MODEL), BF16),
        scratch_shapes=[
            pltpu.VMEM((2, hps, 2, tq, tq), F32),
            pltpu.VMEM((hps, 2, 1, tq), F32),
            pltpu.VMEM((hps, 2, 1, tq), F32),
            pltpu.VMEM((hps, 2, ATTN_V_DIM, tq), F32),
        ],
        compiler_params=_params(3),
        name="diff_attention",
    )(qt, k, vt, pos, slopes, lam_vecs, g_wide)


def _ffn_kernel(tiles_per_seq, fc, y_ref, wo_ref, gmix_ref, x_ref, gpre_ref, wup_ref, cw_ref, cb_ref,
                wdn_ref, gpost_ref, o_ref, act_ref, carry_ref):
    @pl.when(pl.program_id(0) % tiles_per_seq == 0)
    def _():
        carry_ref[...] = jnp.zeros_like(carry_ref)

    x = x_ref[...] + _rms(_dot(y_ref[...], wo_ref[...]), gmix_ref[...])
    hn = _rms(x, gpre_ref[...]).astype(BF16)
    for c in range(D_FF // fc):
        halves = []
        for part in range(2):
            cols = slice(part * D_FF + c * fc, part * D_FF + (c + 1) * fc)
            h = _dot(hn, wup_ref[:, cols])
            halves.append(_causal_conv(h, carry_ref, cw_ref, cb_ref, cols, FFN_CONV, horner=False))
        act_ref[:, c * fc:(c + 1) * fc] = (_silu(halves[0]) * halves[1]).astype(BF16)
    f = _dot(act_ref[...], wdn_ref[...])
    o_ref[...] = x + _rms(f, gpost_ref[...])


def _mixer_out_ffn(y, wo, gmix, x, gpre, wup, conv_w, conv_b, wdn, gpost, layer, mix_layer, seq):
    t, k = y.shape
    tm = min(ROW_TILE, seq)
    fc = FFN_CHUNK
    row = lambda i: (i, 0)
    return pl.pallas_call(
        functools.partial(_ffn_kernel, seq // tm, fc),
        grid=(t // tm,),
        in_specs=[
            pl.BlockSpec((tm, k), row),
            _layer(mix_layer, (k, D_MODEL)),
            _layer(layer, (1, D_MODEL)),
            pl.BlockSpec((tm, D_MODEL), row),
            _layer(layer, (1, D_MODEL)),
            _layer(layer, (D_MODEL, 2 * D_FF)),
            _layer(layer, (FFN_CONV, 2 * D_FF)),
            _layer(layer, (1, 2 * D_FF)),
            _layer(layer, (D_FF, D_MODEL)),
            _layer(layer, (1, D_MODEL)),
        ],
        out_specs=pl.BlockSpec((tm, D_MODEL), row),
        out_shape=jax.ShapeDtypeStruct((t, D_MODEL), F32),
        scratch_shapes=[
            pltpu.VMEM((tm, D_FF), BF16),
            pltpu.VMEM((SUBLANES, 2 * D_FF), F32),
        ],
        compiler_params=_params(1),
        name="mixer_out_ffn",
    )(y, wo, gmix, x, gpre, wup, conv_w, conv_b, wdn, gpost)


def _rows(v):
    return v.astype(F32)[:, None, :]


def _head_expand_matrix():
    head_of_lane = jnp.arange(D_INNER) // SSM_HEAD_DIM
    return (jnp.arange(SSM_HEADS)[:, None] == head_of_lane[None, :]).astype(BF16)


def _alibi_features(tk):
    kk = jnp.arange(tk)
    base = int(POS_DIGIT_BASE)
    cols = jnp.stack([kk // base, kk % base, kk // base, kk % base], axis=1).astype(F32)
    return jnp.pad(cols, ((0, 0), (0, LANES - 4))).astype(BF16)


def kernel(x, ssm_w_in, ssm_conv_w, ssm_conv_b, ssm_dt_bias, ssm_A_log, ssm_D, ssm_norm_g, ssm_w_out, attn_w_qkv, attn_lambda_q1, attn_lambda_k1, attn_lambda_q2, attn_lambda_k2, attn_subln_g, attn_w_out, norm_mix_pre, norm_mix_post, norm_ffn_pre, norm_ffn_post, ffn_w_up, ffn_conv_w, ffn_conv_b, ffn_w_down):
    batch, seq, d_model = x.shape
    depth = norm_mix_pre.shape[0]
    assert d_model == D_MODEL and seq % ROW_TILE == 0 and seq % ATTN_TILE == 0
    t = batch * seq
    xs = x.reshape(t, D_MODEL).astype(F32)

    w_in = jnp.pad(ssm_w_in, ((0, 0), (0, 0), (0, SSM_IN_PAD - SSM_IN_DIM))).astype(BF16)
    conv_w, conv_b = ssm_conv_w.astype(F32), _rows(ssm_conv_b)
    dt_bias, a_log = _rows(ssm_dt_bias), _rows(ssm_A_log)
    d_wide = _rows(jnp.repeat(ssm_D, SSM_HEAD_DIM, axis=1))
    ssm_g = _rows(ssm_norm_g)
    w_out_ssm = ssm_w_out.astype(BF16)
    w_qkv = attn_w_qkv.astype(BF16)
    lam_vecs = jnp.stack([attn_lambda_q1, attn_lambda_k1, attn_lambda_q2, attn_lambda_k2],
                         axis=1).astype(F32)
    g_wide = jnp.broadcast_to(attn_subln_g.astype(F32)[:, :, None],
                              attn_subln_g.shape + (LANES,))
    w_out_attn = attn_w_out.astype(BF16)
    g_mix_pre, g_mix_post = _rows(norm_mix_pre), _rows(norm_mix_post)
    g_ffn_pre, g_ffn_post = _rows(norm_ffn_pre), _rows(norm_ffn_post)
    w_up, w_down = ffn_w_up.astype(BF16), ffn_w_down.astype(BF16)
    f_conv_w, f_conv_b = ffn_conv_w.astype(F32), _rows(ffn_conv_b)

    expand = _head_expand_matrix()
    pos = _alibi_features(min(ATTN_TILE, seq))
    slopes = jnp.exp2(-8.0 * jnp.arange(1, ATTN_HEADS + 1, dtype=F32) / ATTN_HEADS)
    slopes = jnp.broadcast_to(slopes[:, None, None], (ATTN_HEADS, SUBLANES, LANES))

    for i in range(depth):
        j = i // 2
        if i % 2 == 0:
            z, xbc, dt = _ssm_inproj(xs, g_mix_pre, w_in, conv_w, conv_b, i, j, seq)
            mixed = _ssd(xbc, z, dt, dt_bias, a_log, d_wide, ssm_g, expand, j, batch, seq)
            w_o = w_out_ssm
        else:
            lambda_init = 0.8 - 0.6 * math.exp(-0.3 * i)
            qt, k, vt = _qkv(xs, g_mix_pre, w_qkv, i, j, seq)
            mixed = _attention(qt, k, vt, pos, slopes, lam_vecs, g_wide, lambda_init, j, batch, seq)
            w_o = w_out_attn
        xs = _mixer_out_ffn(mixed, w_o, g_mix_post, xs, g_ffn_pre, w_up, f_conv_w, f_conv_b,
                            w_down, g_ffn_post, i, j, seq)
    return xs.reshape(batch, seq, D_MODEL).astype(x.dtype)
```

```python
import functools
import math

import jax
import jax.numpy as jnp
from jax import lax
from jax.experimental import pallas as pl
from jax.experimental.pallas import tpu as pltpu

F32 = jnp.float32
BF16 = jnp.bfloat16

D_MODEL = 1024
D_INNER = 2048
SSM_HEAD_DIM = 64
SSM_HEADS = 32
SSM_GROUPS = 8
SSM_STATE = 128
SSM_CONV = 4
SSM_CHUNK = 128
SSM_BC_DIM = SSM_GROUPS * SSM_STATE
SSM_CONV_DIM = D_INNER + 2 * SSM_BC_DIM
SSM_IN_DIM = D_INNER + SSM_CONV_DIM + SSM_HEADS
SSM_IN_PAD = D_INNER + SSM_CONV_DIM + 128
HEADS_PER_GROUP = SSM_HEADS // SSM_GROUPS
GROUP_WIDTH = D_INNER // SSM_GROUPS
ATTN_HEAD_DIM = 64
ATTN_HEADS = 8
ATTN_V_DIM = 128
D_FF = 2816
FFN_CONV = 3
RMS_EPS = 1e-6
LOG2_E = math.log2(math.e)
POS_DIGIT_BASE = 16.0

LANES = 128
SUBLANES = 8
V7X_VMEM_LIMIT_BYTES = 56 * 1024 * 1024

ROW_TILE = 512
INPROJ_ROW_TILE = 1024
ATTN_TILE = 512
N_CHUNK = 512
FFN_CHUNK = 256
SSD_CHUNKS_PER_STEP = 4
ATTN_HEADS_PER_STEP = 4


def _params(n_axes):
    return pltpu.CompilerParams(
        dimension_semantics=("arbitrary",) * n_axes,
        vmem_limit_bytes=V7X_VMEM_LIMIT_BYTES,
    )


def _resident(shape):
    zeros = (0,) * len(shape)
    return pl.BlockSpec(shape, lambda *_: zeros, pipeline_mode=pl.Buffered(1))


def _layer(index, shape):
    idx = (index,) + (0,) * len(shape)
    return pl.BlockSpec((None,) + tuple(shape), lambda *_: idx, pipeline_mode=pl.Buffered(1))


def _rms(x, g):
    return x * lax.rsqrt(jnp.mean(x * x, axis=-1, keepdims=True) + RMS_EPS) * g


def _silu(x):
    half = 0.5 * x
    return half + half * jnp.tanh(half)


def _dot(a, b):
    return jnp.dot(a, b, preferred_element_type=F32)


def _dot_nt(a, b):
    return lax.dot_general(a, b, (((1,), (1,)), ((), ())), preferred_element_type=F32)


def _causal_conv(h, carry_ref, w_ref, b_ref, cols, width, horner):
    rows, c = h.shape
    n = rows // SUBLANES + 1
    prev = carry_ref[:, cols]
    carry_ref[:, cols] = h[rows - SUBLANES:rows, :]
    tiles = jnp.concatenate([prev, h], axis=0).reshape(n, SUBLANES, c)
    sub = lax.broadcasted_iota(jnp.int32, (n, SUBLANES, c), 1)

    def shift(a, k):
        rot = pltpu.roll(a, k, 1)
        above = jnp.concatenate([rot[:1], rot[:-1]], axis=0)
        return jnp.where(sub < k, above, rot)

    def tap(k):
        return tiles * w_ref[k:k + 1, cols].reshape(1, 1, c)

    if horner:
        a = tap(0)
        for k in range(1, width):
            a = shift(a, 1) + tap(k)
    else:
        a = tap(width - 1)
        for k in range(1, width):
            a = a + shift(tiles, k) * w_ref[width - 1 - k:width - k, cols].reshape(1, 1, c)
    return a[1:].reshape(rows, c) + b_ref[:, cols]


def _ssm_inproj_kernel(tiles_per_seq, x_ref, g_ref, w_ref, cw_ref, cb_ref,
                       z_ref, xbc_ref, dt_ref, carry_ref):
    @pl.when(pl.program_id(0) % tiles_per_seq == 0)
    def _():
        carry_ref[...] = jnp.zeros_like(carry_ref)

    hn = _rms(x_ref[...], g_ref[...]).astype(BF16)
    dt_ref[...] = _dot(hn, w_ref[:, D_INNER + SSM_CONV_DIM:])
    n_chunks = SSM_CONV_DIM // N_CHUNK
    zc = D_INNER // n_chunks
    for c in range(n_chunks):
        cols = slice(c * N_CHUNK, (c + 1) * N_CHUNK)
        h = _dot(hn, w_ref[:, D_INNER + c * N_CHUNK:D_INNER + (c + 1) * N_CHUNK])
        zcols = slice(c * zc, (c + 1) * zc)
        z_ref[:, zcols] = _dot(hn, w_ref[:, zcols]).astype(BF16)
        y = _causal_conv(h, carry_ref, cw_ref, cb_ref, cols, SSM_CONV, horner=True)
        xbc_ref[:, cols] = _silu(y).astype(BF16)


def _ssm_inproj(x, g, w_in, conv_w, conv_b, layer, ssm_layer, seq):
    t = x.shape[0]
    tm = min(INPROJ_ROW_TILE, seq)
    row = lambda i: (i, 0)
    return pl.pallas_call(
        functools.partial(_ssm_inproj_kernel, seq // tm),
        grid=(t // tm,),
        in_specs=[
            pl.BlockSpec((tm, D_MODEL), row),
            _layer(layer, (1, D_MODEL)),
            _layer(ssm_layer, (D_MODEL, SSM_IN_PAD)),
            _layer(ssm_layer, (SSM_CONV, SSM_CONV_DIM)),
            _layer(ssm_layer, (1, SSM_CONV_DIM)),
        ],
        out_specs=[
            pl.BlockSpec((tm, D_INNER), row),
            pl.BlockSpec((tm, SSM_CONV_DIM), row),
            pl.BlockSpec((tm, LANES), row),
        ],
        out_shape=[
            jax.ShapeDtypeStruct((t, D_INNER), BF16),
            jax.ShapeDtypeStruct((t, SSM_CONV_DIM), BF16),
            jax.ShapeDtypeStruct((t, LANES), F32),
        ],
        scratch_shapes=[pltpu.VMEM((SUBLANES, SSM_CONV_DIM), F32)],
        compiler_params=_params(1),
        name="ssm_inproj",
    )(x, g, w_in, conv_w, conv_b)


def _ssd_chunk(rows, xbc_ref, z_ref, dt_ref, dtb_ref, alog_ref, dexp_ref, ng_ref, e_ref,
               y_ref, state_ref):
    L = SSM_CHUNK
    H = SSM_HEADS

    dt_raw = dt_ref[rows, 0:H] + dtb_ref[...]
    dt = jnp.maximum(dt_raw, 0.0) + jnp.log(1.0 + jnp.exp(-jnp.abs(dt_raw)))
    acs = dt * (-jnp.exp(alog_ref[...]) * LOG2_E)
    row_h = lax.broadcasted_iota(jnp.int32, (L, H), 0)
    k = 1
    while k < L:
        acs = acs + jnp.where(row_h >= k, pltpu.roll(acs, k, 0), 0.0)
        k *= 2
    acs_t = acs.T
    rowp_t = acs_t - jnp.log2(dt.T)
    w_t = jnp.exp2(acs_t[:, L - 1:L] - rowp_t)

    cd = jnp.broadcast_to(jnp.exp2(acs[L - 1:L, :]), (SUBLANES, H))
    cd_hi = cd.astype(BF16)
    cd_lo = (cd - cd_hi.astype(F32)).astype(BF16)
    cd_wide = (_dot(cd_hi, e_ref[...]) + _dot(cd_lo, e_ref[...]))[0:1, :]

    row = lax.broadcasted_iota(jnp.int32, (L, LANES), 0)
    lane = lax.broadcasted_iota(jnp.int32, (L, LANES), 1)
    causal = row >= lane
    keep_l = jnp.where(lane < SSM_HEAD_DIM, 1.0, 0.0).astype(BF16)
    keep_r = jnp.where(lane < SSM_HEAD_DIM, 0.0, 1.0).astype(BF16)

    for g in range(SSM_GROUPS):
        b_g = xbc_ref[rows, D_INNER + g * SSM_STATE:D_INNER + (g + 1) * SSM_STATE]
        c_g = xbc_ref[rows, D_INNER + SSM_BC_DIM + g * SSM_STATE:
                      D_INNER + SSM_BC_DIM + (g + 1) * SSM_STATE]
        cb = _dot_nt(c_g, b_g)
        b_t = b_g.astype(F32).T
        c_f = c_g.astype(F32)
        ys = []
        for pr in range(HEADS_PER_GROUP // 2):
            heads = (g * HEADS_PER_GROUP + 2 * pr, g * HEADS_PER_GROUP + 2 * pr + 1)
            cols = slice((g * 2 + pr) * LANES, (g * 2 + pr + 1) * LANES)
            scols = slice(pr * LANES, (pr + 1) * LANES)
            xp = xbc_ref[rows, cols]
            sp = state_ref[g, :, scols]
            m_parts, c_parts, bw_parts = [], [], []
            for h in heads:
                a_col = jnp.broadcast_to(acs[:, h:h + 1], (L, L))
                a_row = jnp.broadcast_to(rowp_t[h:h + 1, :], (L, L))
                m_parts.append(
                    (cb * jnp.exp2(jnp.where(causal, a_col - a_row, -jnp.inf))).astype(BF16))
                c_parts.append((c_f * jnp.exp2(a_col)).astype(BF16))
                bw_parts.append((b_t * jnp.broadcast_to(w_t[h:h + 1, :], (L, L))).astype(BF16))
            sb = sp.astype(BF16)
            x_blk = jnp.concatenate([xp * keep_l, xp * keep_r], axis=0)
            y_p = _dot(jnp.concatenate(m_parts + c_parts, axis=1),
                       jnp.concatenate([x_blk, sb * keep_l, sb * keep_r], axis=0))
            y_p = y_p + xp.astype(F32) * dexp_ref[:, cols]
            state_ref[g, :, scols] = sp * cd_wide[:, cols] + _dot(
                jnp.concatenate(bw_parts, axis=1), x_blk)
            ys.append(y_p)
        gcols = slice(g * GROUP_WIDTH, (g + 1) * GROUP_WIDTH)
        y_g = jnp.concatenate(ys, axis=1) * _silu(z_ref[rows, gcols].astype(F32))
        y_ref[rows, gcols] = _rms(y_g, ng_ref[:, gcols]).astype(BF16)


def _ssd_kernel(*refs):
    state_ref = refs[-1]

    @pl.when(pl.program_id(1) == 0)
    def _():
        state_ref[...] = jnp.zeros_like(state_ref)

    for sub in range(SSD_CHUNKS_PER_STEP):
        _ssd_chunk(slice(sub * SSM_CHUNK, (sub + 1) * SSM_CHUNK), *refs)


def _ssd(xbc, z, dt, dt_bias, a_log, d_wide, norm_g, expand, ssm_layer, batch, seq):
    t = xbc.shape[0]
    rows = SSD_CHUNKS_PER_STEP * SSM_CHUNK
    n_steps = seq // rows
    row = lambda b, c: (b * n_steps + c, 0)
    return pl.pallas_call(
        _ssd_kernel,
        grid=(batch, n_steps),
        in_specs=[
            pl.BlockSpec((rows, SSM_CONV_DIM), row),
            pl.BlockSpec((rows, D_INNER), row),
            pl.BlockSpec((rows, LANES), row),
            _layer(ssm_layer, (1, SSM_HEADS)),
            _layer(ssm_layer, (1, SSM_HEADS)),
            _layer(ssm_layer, (1, D_INNER)),
            _layer(ssm_layer, (1, D_INNER)),
            _resident((SSM_HEADS, D_INNER)),
        ],
        out_specs=pl.BlockSpec((rows, D_INNER), row),
        out_shape=jax.ShapeDtypeStruct((t, D_INNER), BF16),
        scratch_shapes=[pltpu.VMEM((SSM_GROUPS, SSM_STATE, GROUP_WIDTH), F32)],
        compiler_params=_params(2),
        name="ssd_scan",
    )(xbc, z, dt, dt_bias, a_log, d_wide, norm_g, expand)


def _qkv_kernel(x_ref, g_ref, w_ref, qt_ref, k_ref, vt_ref, wqt_ref, wvt_ref):
    @pl.when(pl.program_id(0) == 0)
    def _():
        wqt_ref[...] = w_ref[:, 0:D_MODEL].T
        wvt_ref[...] = w_ref[:, 2 * D_MODEL:3 * D_MODEL].T

    hn = _rms(x_ref[...], g_ref[...]).astype(BF16)
    scale = ATTN_HEAD_DIM ** -0.5 * LOG2_E
    for c in range(D_MODEL // N_CHUNK):
        rows = slice(c * N_CHUNK, (c + 1) * N_CHUNK)
        qt_ref[0, rows, :] = (_dot_nt(wqt_ref[rows, :], hn) * scale).astype(BF16)
        vt_ref[0, rows, :] = _dot_nt(wvt_ref[rows, :], hn).astype(BF16)
        k_ref[:, rows] = _dot(hn, w_ref[:, D_MODEL + c * N_CHUNK:D_MODEL + (c + 1) * N_CHUNK]
                              ).astype(BF16)


def _qkv(x, g, w_qkv, layer, attn_layer, seq):
    t = x.shape[0]
    tm = min(ATTN_TILE, seq)
    row = lambda i: (i, 0)
    tile = lambda i: (i, 0, 0)
    return pl.pallas_call(
        _qkv_kernel,
        grid=(t // tm,),
        in_specs=[
            pl.BlockSpec((tm, D_MODEL), row),
            _layer(layer, (1, D_MODEL)),
            _layer(attn_layer, (D_MODEL, 3 * D_MODEL)),
        ],
        out_specs=[
            pl.BlockSpec((1, D_MODEL, tm), tile),
            pl.BlockSpec((tm, D_MODEL), row),
            pl.BlockSpec((1, D_MODEL, tm), tile),
        ],
        out_shape=[
            jax.ShapeDtypeStruct((t // tm, D_MODEL, tm), BF16),
            jax.ShapeDtypeStruct((t, D_MODEL), BF16),
            jax.ShapeDtypeStruct((t // tm, D_MODEL, tm), BF16),
        ],
        scratch_shapes=[
            pltpu.VMEM((D_MODEL, D_MODEL), BF16),
            pltpu.VMEM((D_MODEL, D_MODEL), BF16),
        ],
        compiler_params=_params(1),
        name="attn_qkv",
    )(x, g, w_qkv)


def _attn_kernel(lambda_init, qt_ref, k_ref, vt_ref, pos_ref, slope_ref, lam_ref, g_ref,
                 o_ref, s_ref, m_ref, l_ref, acc_ref):
    tq = qt_ref.shape[2]
    tk = tq
    qi = pl.program_id(2)
    heads = range(ATTN_HEADS_PER_STEP)

    r8 = lax.broadcasted_iota(jnp.int32, (SUBLANES, tq), 0)
    zero_half = jnp.zeros((ATTN_HEAD_DIM, tq), BF16)
    zero_pad = jnp.zeros((LANES - SUBLANES, tq), BF16)
    q_aug, slope_row = [], []
    for hh in heads:
        slope = jnp.concatenate([slope_ref[hh]] * (tq // LANES), axis=1) * LOG2_E
        s_hi = slope.astype(BF16).astype(F32)
        s_lo = slope - s_hi
        feat = jnp.where(r8 == 0, POS_DIGIT_BASE * s_hi,
                         jnp.where(r8 == 1, s_hi,
                                   jnp.where(r8 == 2, POS_DIGIT_BASE * s_lo,
                                             jnp.where(r8 == 3, s_lo, 0.0)))).astype(BF16)
        aug = jnp.concatenate([feat, zero_pad], axis=0)
        q_all = qt_ref[0, hh * ATTN_V_DIM:(hh + 1) * ATTN_V_DIM, :]
        q_aug.append((jnp.concatenate([q_all[0:ATTN_HEAD_DIM], zero_half, aug], axis=0),
                      jnp.concatenate([zero_half, q_all[ATTN_HEAD_DIM:], aug], axis=0)))
        slope_row.append(slope[0:1, :])
    pos = pos_ref[...]

    m_ref[...] = jnp.full(m_ref.shape, -jnp.inf, F32)
    l_ref[...] = jnp.zeros_like(l_ref)
    acc_ref[...] = jnp.zeros_like(acc_ref)

    def scores(ki, slot):
        rows = pl.ds(pl.multiple_of(ki * tk, tk), tk)
        for hh in heads:
            k_aug = jnp.concatenate(
                [k_ref[rows, hh * ATTN_V_DIM:(hh + 1) * ATTN_V_DIM], pos], axis=1)
            for j in range(2):
                s_ref[slot, hh, j] = _dot(k_aug, q_aug[hh][j])

    def softmax_pv(ki, slot, mask):
        for hh in heads:
            v_blk = vt_ref[ki, hh * ATTN_V_DIM:(hh + 1) * ATTN_V_DIM, :]
            c_blk = slope_row[hh] * ((ki - qi) * tk).astype(F32)
            for j in range(2):
                s = s_ref[slot, hh, j]
                if mask is not None:
                    s = jnp.where(mask, s, -jnp.inf)
                m_old = m_ref[hh, j]
                m_new = jnp.maximum(m_old, jnp.max(s, axis=0, keepdims=True) + c_blk)
                p = jnp.exp2(s - (m_new - c_blk))
                alpha = jnp.exp2(m_old - m_new)
                l_ref[hh, j] = alpha * l_ref[hh, j] + jnp.sum(p, axis=0, keepdims=True)
                acc_ref[hh, j] = alpha * acc_ref[hh, j] + _dot(v_blk, p.astype(BF16))
                m_ref[hh, j] = m_new

    kk = lax.broadcasted_iota(jnp.int32, (tk, tq), 0)
    qq = lax.broadcasted_iota(jnp.int32, (tk, tq), 1)
    causal = kk <= qq

    @pl.when(qi == 0)
    def _():
        scores(0, 0)
        softmax_pv(0, 0, causal)

    @pl.when(qi > 0)
    def _():
        scores(0, 0)
        scores(1, 1)
        softmax_pv(0, 0, None)

        def body(i, carry):
            b0 = 2 * i + 1
            scores(b0 + 1, 0)
            softmax_pv(b0, 1, None)
            scores(b0 + 2, 1)
            softmax_pv(b0 + 1, 0, None)
            return carry

        lax.fori_loop(0, (qi - 1) // 2, body, 0)

        @pl.when(qi % 2 == 1)
        def _():
            softmax_pv(qi, 1, causal)

        @pl.when(qi % 2 == 0)
        def _():
            scores(qi, 0)
            softmax_pv(qi - 1, 1, None)
            softmax_pv(qi, 0, causal)

    lv = lam_ref[...]
    lam = (jnp.exp(jnp.sum(lv[0:1] * lv[1:2], axis=-1, keepdims=True))
           - jnp.exp(jnp.sum(lv[2:3] * lv[3:4], axis=-1, keepdims=True)) + lambda_init)
    g_col = jnp.concatenate([g_ref[...]] * (tq // LANES), axis=1)
    for hh in heads:
        o = acc_ref[hh, 0] / l_ref[hh, 0] - lam * (acc_ref[hh, 1] / l_ref[hh, 1])
        o = o * lax.rsqrt(jnp.mean(o * o, axis=0, keepdims=True) + RMS_EPS) * g_col
        o_ref[:, hh * ATTN_V_DIM:(hh + 1) * ATTN_V_DIM] = (
            o * (1.0 - lambda_init)).T.astype(BF16)


def _attention(qt, k, vt, pos, slopes, lam_vecs, g_wide, lambda_init, attn_layer, batch, seq):
    t = k.shape[0]
    tq = qt.shape[2]
    nq = seq // tq
    hps = ATTN_HEADS_PER_STEP
    width = hps * ATTN_V_DIM
    return pl.pallas_call(
        functools.partial(_attn_kernel, lambda_init),
        grid=(batch, ATTN_HEADS // hps, nq),
        in_specs=[
            pl.BlockSpec((1, width, tq), lambda b, h, q: (b * nq + q, h, 0)),
            pl.BlockSpec((seq, width), lambda b, h, q: (b, h)),
            pl.BlockSpec((nq, width, tq), lambda b, h, q: (b, h, 0)),
            _resident((tq, LANES)),
            pl.BlockSpec((hps, SUBLANES, LANES), lambda b, h, q: (h, 0, 0)),
            _layer(attn_layer, (4, ATTN_HEAD_DIM)),
            _layer(attn_layer, (ATTN_V_DIM, LANES)),
        ],
        out_specs=pl.BlockSpec((tq, width), lambda b, h, q: (b * nq + q, h)),
        out_shape=jax.ShapeDtypeStruct((t, D_MODEL), BF16),
        scratch_shapes=[
            pltpu.VMEM((2, hps, 2, tq, tq), F32),
            pltpu.VMEM((hps, 2, 1, tq), F32),
            pltpu.VMEM((hps, 2, 1, tq), F32),
            pltpu.VMEM((hps, 2, ATTN_V_DIM, tq), F32),
        ],
        compiler_params=_params(3),
        name="diff_attention",
    )(qt, k, vt, pos, slopes, lam_vecs, g_wide)


def _ffn_kernel(tiles_per_seq, fc, y_ref, wo_ref, gmix_ref, x_ref, gpre_ref, wup_ref, cw_ref, cb_ref,
                wdn_ref, gpost_ref, o_ref, act_ref, carry_ref):
    @pl.when(pl.program_id(0) % tiles_per_seq == 0)
    def _():
        carry_ref[...] = jnp.zeros_like(carry_ref)

    x = x_ref[...] + _rms(_dot(y_ref[...], wo_ref[...]), gmix_ref[...])
    hn = _rms(x, gpre_ref[...]).astype(BF16)
    for c in range(D_FF // fc):
        halves = []
        for part in range(2):
            cols = slice(part * D_FF + c * fc, part * D_FF + (c + 1) * fc)
            h = _dot(hn, wup_ref[:, cols])
            halves.append(_causal_conv(h, carry_ref, cw_ref, cb_ref, cols, FFN_CONV, horner=False))
        act_ref[:, c * fc:(c + 1) * fc] = (_silu(halves[0]) * halves[1]).astype(BF16)
    f = _dot(act_ref[...], wdn_ref[...])
    o_ref[...] = x + _rms(f, gpost_ref[...])


def _mixer_out_ffn(y, wo, gmix, x, gpre, wup, conv_w, conv_b, wdn, gpost, layer, mix_layer, seq):
    t, k = y.shape
    tm = min(ROW_TILE, seq)
    fc = FFN_CHUNK
    row = lambda i: (i, 0)
    return pl.pallas_call(
        functools.partial(_ffn_kernel, seq // tm, fc),
        grid=(t // tm,),
        in_specs=[
            pl.BlockSpec((tm, k), row),
            _layer(mix_layer, (k, D_MODEL)),
            _layer(layer, (1, D_MODEL)),
            pl.BlockSpec((tm, D_MODEL), row),
            _layer(layer, (1, D_MODEL)),
            _layer(layer, (D_MODEL, 2 * D_FF)),
            _layer(layer, (FFN_CONV, 2 * D_FF)),
            _layer(layer, (1, 2 * D_FF)),
            _layer(layer, (D_FF, D_MODEL)),
            _layer(layer, (1, D_MODEL)),
        ],
        out_specs=pl.BlockSpec((tm, D_MODEL), row),
        out_shape=jax.ShapeDtypeStruct((t, D_MODEL), F32),
        scratch_shapes=[
            pltpu.VMEM((tm, D_FF), BF16),
            pltpu.VMEM((SUBLANES, 2 * D_FF), F32),
        ],
        compiler_params=_params(1),
        name="mixer_out_ffn",
    )(y, wo, gmix, x, gpre, wup, conv_w, conv_b, wdn, gpost)


def _rows(v):
    return v.astype(F32)[:, None, :]


def _head_expand_matrix():
    head_of_lane = jnp.arange(D_INNER) // SSM_HEAD_DIM
    return (jnp.arange(SSM_HEADS)[:, None] == head_of_lane[None, :]).astype(BF16)


def _alibi_features(tk):
    kk = jnp.arange(tk)
    base = int(POS_DIGIT_BASE)
    cols = jnp.stack([kk // base, kk % base, kk // base, kk % base], axis=1).astype(F32)
    return jnp.pad(cols, ((0, 0), (0, LANES - 4))).astype(BF16)


def kernel(x, ssm_w_in, ssm_conv_w, ssm_conv_b, ssm_dt_bias, ssm_A_log, ssm_D, ssm_norm_g, ssm_w_out, attn_w_qkv, attn_lambda_q1, attn_lambda_k1, attn_lambda_q2, attn_lambda_k2, attn_subln_g, attn_w_out, norm_mix_pre, norm_mix_post, norm_ffn_pre, norm_ffn_post, ffn_w_up, ffn_conv_w, ffn_conv_b, ffn_w_down):
    batch, seq, d_model = x.shape
    depth = norm_mix_pre.shape[0]
    assert d_model == D_MODEL and seq % ROW_TILE == 0 and seq % ATTN_TILE == 0
    assert seq % min(INPROJ_ROW_TILE, seq) == 0 and seq % (SSD_CHUNKS_PER_STEP * SSM_CHUNK) == 0
    t = batch * seq
    xs = x.reshape(t, D_MODEL).astype(F32)

    w_in = jnp.pad(ssm_w_in, ((0, 0), (0, 0), (0, SSM_IN_PAD - SSM_IN_DIM))).astype(BF16)
    conv_w, conv_b = ssm_conv_w.astype(F32), _rows(ssm_conv_b)
    dt_bias, a_log = _rows(ssm_dt_bias), _rows(ssm_A_log)
    d_wide = _rows(jnp.repeat(ssm_D, SSM_HEAD_DIM, axis=1))
    ssm_g = _rows(ssm_norm_g)
    w_out_ssm = ssm_w_out.astype(BF16)
    w_qkv = attn_w_qkv.astype(BF16)
    lam_vecs = jnp.stack([attn_lambda_q1, attn_lambda_k1, attn_lambda_q2, attn_lambda_k2],
                         axis=1).astype(F32)
    g_wide = jnp.broadcast_to(attn_subln_g.astype(F32)[:, :, None],
                              attn_subln_g.shape + (LANES,))
    w_out_attn = attn_w_out.astype(BF16)
    g_mix_pre, g_mix_post = _rows(norm_mix_pre), _rows(norm_mix_post)
    g_ffn_pre, g_ffn_post = _rows(norm_ffn_pre), _rows(norm_ffn_post)
    w_up, w_down = ffn_w_up.astype(BF16), ffn_w_down.astype(BF16)
    f_conv_w, f_conv_b = ffn_conv_w.astype(F32), _rows(ffn_conv_b)

    expand = _head_expand_matrix()
    pos = _alibi_features(min(ATTN_TILE, seq))
    slopes = jnp.exp2(-8.0 * jnp.arange(1, ATTN_HEADS + 1, dtype=F32) / ATTN_HEADS)
    slopes = jnp.broadcast_to(slopes[:, None, None], (ATTN_HEADS, SUBLANES, LANES))

    for i in range(depth):
        j = i // 2
        if i % 2 == 0:
            z, xbc, dt = _ssm_inproj(xs, g_mix_pre, w_in, conv_w, conv_b, i, j, seq)
            mixed = _ssd(xbc, z, dt, dt_bias, a_log, d_wide, ssm_g, expand, j, batch, seq)
            w_o = w_out_ssm
        else:
            lambda_init = 0.8 - 0.6 * math.exp(-0.3 * i)
            qt, k, vt = _qkv(xs, g_mix_pre, w_qkv, i, j, seq)
            mixed = _attention(qt, k, vt, pos, slopes, lam_vecs, g_wide, lambda_init, j, batch, seq)
            w_o = w_out_attn
        xs = _mixer_out_ffn(mixed, w_o, g_mix_post, xs, g_ffn_pre, w_up, f_conv_w, f_conv_b,
                            w_down, g_ffn_post, i, j, seq)
    return xs.reshape(batch, seq, D_MODEL).astype(x.dtype)
```

```python
import functools
import math

import jax
import jax.numpy as jnp
from jax import lax
from jax.experimental import pallas as pl
from jax.experimental.pallas import tpu as pltpu

F32 = jnp.float32
BF16 = jnp.bfloat16

D_MODEL = 1024
D_INNER = 2048
SSM_HEAD_DIM = 64
SSM_HEADS = 32
SSM_GROUPS = 8
SSM_STATE = 128
SSM_CONV = 4
SSM_CHUNK = 128
SSM_BC_DIM = SSM_GROUPS * SSM_STATE
SSM_CONV_DIM = D_INNER + 2 * SSM_BC_DIM
HEADS_PER_GROUP = SSM_HEADS // SSM_GROUPS
GROUP_WIDTH = D_INNER // SSM_GROUPS
ATTN_HEAD_DIM = 64
ATTN_HEADS = 8
ATTN_V_DIM = 128
D_FF = 2816
FFN_CONV = 3
RMS_EPS = 1e-6
LOG2_E = math.log2(math.e)
POS_DIGIT_BASE = 16.0

LANES = 128
SUBLANES = 8
V7X_VMEM_LIMIT_BYTES = 56 * 1024 * 1024

ROW_TILE = 512
INPROJ_ROW_TILE = 1024
ATTN_TILE = 512
N_CHUNK = 512
FFN_CHUNK = 256
SSD_CHUNKS_PER_STEP = 4
ATTN_HEADS_PER_STEP = 4


def _params(n_axes):
    return pltpu.CompilerParams(
        dimension_semantics=("arbitrary",) * n_axes,
        vmem_limit_bytes=V7X_VMEM_LIMIT_BYTES,
    )


def _resident(shape):
    zeros = (0,) * len(shape)
    return pl.BlockSpec(shape, lambda *_: zeros, pipeline_mode=pl.Buffered(1))


def _layer(index, shape):
    idx = (index,) + (0,) * len(shape)
    return pl.BlockSpec((None,) + tuple(shape), lambda *_: idx, pipeline_mode=pl.Buffered(1))


def _rms(x, g):
    return x * lax.rsqrt(jnp.mean(x * x, axis=-1, keepdims=True) + RMS_EPS) * g


def _silu(x):
    half = 0.5 * x
    return half + half * jnp.tanh(half)


def _dot(a, b):
    return jnp.dot(a, b, preferred_element_type=F32)


def _dot_nt(a, b):
    return lax.dot_general(a, b, (((1,), (1,)), ((), ())), preferred_element_type=F32)


def _causal_conv(h, carry_ref, w_ref, b_ref, cols, width, horner):
    rows, c = h.shape
    n = rows // SUBLANES + 1
    prev = carry_ref[:, cols]
    carry_ref[:, cols] = h[rows - SUBLANES:rows, :]
    tiles = jnp.concatenate([prev, h], axis=0).reshape(n, SUBLANES, c)
    sub = lax.broadcasted_iota(jnp.int32, (n, SUBLANES, c), 1)

    def shift(a, k):
        rot = pltpu.roll(a, k, 1)
        above = jnp.concatenate([rot[:1], rot[:-1]], axis=0)
        return jnp.where(sub < k, above, rot)

    def tap(k):
        return tiles * w_ref[k:k + 1, cols].reshape(1, 1, c)

    if horner:
        a = tap(0)
        for k in range(1, width):
            a = shift(a, 1) + tap(k)
    else:
        a = tap(width - 1)
        for k in range(1, width):
            a = a + shift(tiles, k) * w_ref[width - 1 - k:width - k, cols].reshape(1, 1, c)
    return a[1:].reshape(rows, c) + b_ref[:, cols]


def _ssm_inproj_kernel(tiles_per_seq, x_ref, g_ref, w_ref, wdt_ref, cw_ref, cb_ref,
                       z_ref, xbc_ref, dt_ref, carry_ref):
    @pl.when(pl.program_id(0) % tiles_per_seq == 0)
    def _():
        carry_ref[...] = jnp.zeros_like(carry_ref)

    hn = _rms(x_ref[...], g_ref[...]).astype(BF16)
    dt_ref[...] = _dot(hn, wdt_ref[...])
    n_chunks = SSM_CONV_DIM // N_CHUNK
    zc = D_INNER // n_chunks
    for c in range(n_chunks):
        cols = slice(c * N_CHUNK, (c + 1) * N_CHUNK)
        h = _dot(hn, w_ref[:, D_INNER + c * N_CHUNK:D_INNER + (c + 1) * N_CHUNK])
        zcols = slice(c * zc, (c + 1) * zc)
        z_ref[:, zcols] = _dot(hn, w_ref[:, zcols]).astype(BF16)
        y = _causal_conv(h, carry_ref, cw_ref, cb_ref, cols, SSM_CONV, horner=True)
        xbc_ref[:, cols] = _silu(y).astype(BF16)


def _ssm_inproj(x, g, w_zx, w_dt, conv_w, conv_b, layer, ssm_layer, seq):
    t = x.shape[0]
    tm = min(INPROJ_ROW_TILE, seq)
    row = lambda i: (i, 0)
    return pl.pallas_call(
        functools.partial(_ssm_inproj_kernel, seq // tm),
        grid=(t // tm,),
        in_specs=[
            pl.BlockSpec((tm, D_MODEL), row),
            _layer(layer, (1, D_MODEL)),
            _layer(ssm_layer, (D_MODEL, D_INNER + SSM_CONV_DIM)),
            _layer(ssm_layer, (D_MODEL, LANES)),
            _layer(ssm_layer, (SSM_CONV, SSM_CONV_DIM)),
            _layer(ssm_layer, (1, SSM_CONV_DIM)),
        ],
        out_specs=[
            pl.BlockSpec((tm, D_INNER), row),
            pl.BlockSpec((tm, SSM_CONV_DIM), row),
            pl.BlockSpec((tm, LANES), row),
        ],
        out_shape=[
            jax.ShapeDtypeStruct((t, D_INNER), BF16),
            jax.ShapeDtypeStruct((t, SSM_CONV_DIM), BF16),
            jax.ShapeDtypeStruct((t, LANES), F32),
        ],
        scratch_shapes=[pltpu.VMEM((SUBLANES, SSM_CONV_DIM), F32)],
        compiler_params=_params(1),
        name="ssm_inproj",
    )(x, g, w_zx, w_dt, conv_w, conv_b)


def _ssd_chunk(rows, xbc_ref, z_ref, dt_ref, dtb_ref, alog_ref, dexp_ref, ng_ref, e_ref,
               y_ref, state_ref):
    L = SSM_CHUNK
    H = SSM_HEADS

    dt_raw = dt_ref[rows, 0:H] + dtb_ref[...]
    dt = jnp.maximum(dt_raw, 0.0) + jnp.log(1.0 + jnp.exp(-jnp.abs(dt_raw)))
    acs = dt * (-jnp.exp(alog_ref[...]) * LOG2_E)
    row_h = lax.broadcasted_iota(jnp.int32, (L, H), 0)
    k = 1
    while k < L:
        acs = acs + jnp.where(row_h >= k, pltpu.roll(acs, k, 0), 0.0)
        k *= 2
    acs_t = acs.T
    rowp_t = acs_t - jnp.log2(dt.T)
    w_t = jnp.exp2(acs_t[:, L - 1:L] - rowp_t)

    ea = jnp.exp2(acs)
    ea_hi = ea.astype(BF16)
    ea_lo = (ea - ea_hi.astype(F32)).astype(BF16)
    ea_wide = _dot(ea_hi, e_ref[...]) + _dot(ea_lo, e_ref[...])
    cd_wide = ea_wide[L - 1:L, :]

    row = lax.broadcasted_iota(jnp.int32, (L, LANES), 0)
    lane = lax.broadcasted_iota(jnp.int32, (L, LANES), 1)
    causal = row >= lane
    keep_l = jnp.where(lane < SSM_HEAD_DIM, 1.0, 0.0).astype(BF16)
    keep_r = jnp.where(lane < SSM_HEAD_DIM, 0.0, 1.0).astype(BF16)

    for g in range(SSM_GROUPS):
        b_g = xbc_ref[rows, D_INNER + g * SSM_STATE:D_INNER + (g + 1) * SSM_STATE]
        c_g = xbc_ref[rows, D_INNER + SSM_BC_DIM + g * SSM_STATE:
                      D_INNER + SSM_BC_DIM + (g + 1) * SSM_STATE]
        cb = _dot_nt(c_g, b_g)
        b_t = b_g.astype(F32).T
        gcols = slice(g * GROUP_WIDTH, (g + 1) * GROUP_WIDTH)
        y_off = _dot(c_g, state_ref[g].astype(BF16)) * ea_wide[:, gcols]
        ys = []
        for pr in range(HEADS_PER_GROUP // 2):
            heads = (g * HEADS_PER_GROUP + 2 * pr, g * HEADS_PER_GROUP + 2 * pr + 1)
            cols = slice((g * 2 + pr) * LANES, (g * 2 + pr + 1) * LANES)
            scols = slice(pr * LANES, (pr + 1) * LANES)
            xp = xbc_ref[rows, cols]
            sp = state_ref[g, :, scols]
            m_parts, bw_parts = [], []
            for h in heads:
                a_col = jnp.broadcast_to(acs[:, h:h + 1], (L, L))
                a_row = jnp.broadcast_to(rowp_t[h:h + 1, :], (L, L))
                m_parts.append(
                    (cb * jnp.exp2(jnp.where(causal, a_col - a_row, -jnp.inf))).astype(BF16))
                bw_parts.append((b_t * jnp.broadcast_to(w_t[h:h + 1, :], (L, L))).astype(BF16))
            x_blk = jnp.concatenate([xp * keep_l, xp * keep_r], axis=0)
            y_p = _dot(jnp.concatenate(m_parts, axis=1), x_blk) + y_off[:, scols]
            y_p = y_p + xp.astype(F32) * dexp_ref[:, cols]
            state_ref[g, :, scols] = sp * cd_wide[:, cols] + _dot(
                jnp.concatenate(bw_parts, axis=1), x_blk)
            ys.append(y_p)
        y_g = jnp.concatenate(ys, axis=1) * _silu(z_ref[rows, gcols].astype(F32))
        y_ref[rows, gcols] = _rms(y_g, ng_ref[:, gcols]).astype(BF16)


def _ssd_kernel(*refs):
    state_ref = refs[-1]

    @pl.when(pl.program_id(1) == 0)
    def _():
        state_ref[...] = jnp.zeros_like(state_ref)

    for sub in range(SSD_CHUNKS_PER_STEP):
        _ssd_chunk(slice(sub * SSM_CHUNK, (sub + 1) * SSM_CHUNK), *refs)


def _ssd(xbc, z, dt, dt_bias, a_log, d_wide, norm_g, expand, ssm_layer, batch, seq):
    t = xbc.shape[0]
    rows = SSD_CHUNKS_PER_STEP * SSM_CHUNK
    n_steps = seq // rows
    row = lambda b, c: (b * n_steps + c, 0)
    return pl.pallas_call(
        _ssd_kernel,
        grid=(batch, n_steps),
        in_specs=[
            pl.BlockSpec((rows, SSM_CONV_DIM), row),
            pl.BlockSpec((rows, D_INNER), row),
            pl.BlockSpec((rows, LANES), row),
            _layer(ssm_layer, (1, SSM_HEADS)),
            _layer(ssm_layer, (1, SSM_HEADS)),
            _layer(ssm_layer, (1, D_INNER)),
            _layer(ssm_layer, (1, D_INNER)),
            _resident((SSM_HEADS, D_INNER)),
        ],
        out_specs=pl.BlockSpec((rows, D_INNER), row),
        out_shape=jax.ShapeDtypeStruct((t, D_INNER), BF16),
        scratch_shapes=[pltpu.VMEM((SSM_GROUPS, SSM_STATE, GROUP_WIDTH), F32)],
        compiler_params=_params(2),
        name="ssd_scan",
    )(xbc, z, dt, dt_bias, a_log, d_wide, norm_g, expand)


def _qkv_kernel(x_ref, g_ref, w_ref, qt_ref, k_ref, vt_ref, wqt_ref, wvt_ref):
    @pl.when(pl.program_id(0) == 0)
    def _():
        wqt_ref[...] = w_ref[:, 0:D_MODEL].T
        wvt_ref[...] = w_ref[:, 2 * D_MODEL:3 * D_MODEL].T

    hn = _rms(x_ref[...], g_ref[...]).astype(BF16)
    scale = ATTN_HEAD_DIM ** -0.5 * LOG2_E
    for c in range(D_MODEL // N_CHUNK):
        rows = slice(c * N_CHUNK, (c + 1) * N_CHUNK)
        qt_ref[0, rows, :] = (_dot_nt(wqt_ref[rows, :], hn) * scale).astype(BF16)
        vt_ref[0, rows, :] = _dot_nt(wvt_ref[rows, :], hn).astype(BF16)
        k_ref[:, rows] = _dot(hn, w_ref[:, D_MODEL + c * N_CHUNK:D_MODEL + (c + 1) * N_CHUNK]
                              ).astype(BF16)


def _qkv(x, g, w_qkv, layer, attn_layer, seq):
    t = x.shape[0]
    tm = min(ATTN_TILE, seq)
    row = lambda i: (i, 0)
    tile = lambda i: (i, 0, 0)
    return pl.pallas_call(
        _qkv_kernel,
        grid=(t // tm,),
        in_specs=[
            pl.BlockSpec((tm, D_MODEL), row),
            _layer(layer, (1, D_MODEL)),
            _layer(attn_layer, (D_MODEL, 3 * D_MODEL)),
        ],
        out_specs=[
            pl.BlockSpec((1, D_MODEL, tm), tile),
            pl.BlockSpec((tm, D_MODEL), row),
            pl.BlockSpec((1, D_MODEL, tm), tile),
        ],
        out_shape=[
            jax.ShapeDtypeStruct((t // tm, D_MODEL, tm), BF16),
            jax.ShapeDtypeStruct((t, D_MODEL), BF16),
            jax.ShapeDtypeStruct((t // tm, D_MODEL, tm), BF16),
        ],
        scratch_shapes=[
            pltpu.VMEM((D_MODEL, D_MODEL), BF16),
            pltpu.VMEM((D_MODEL, D_MODEL), BF16),
        ],
        compiler_params=_params(1),
        name="attn_qkv",
    )(x, g, w_qkv)


def _attn_kernel(lambda_init, qt_ref, k_ref, vt_ref, pos_ref, slope_ref, lam_ref, g_ref,
                 o_ref, s_ref, m_ref, l_ref, acc_ref):
    tq = qt_ref.shape[2]
    tk = tq
    qi = pl.program_id(2)
    heads = range(ATTN_HEADS_PER_STEP)

    r8 = lax.broadcasted_iota(jnp.int32, (SUBLANES, tq), 0)
    zero_half = jnp.zeros((ATTN_HEAD_DIM, tq), BF16)
    zero_pad = jnp.zeros((LANES - SUBLANES, tq), BF16)
    q_aug, slope_row = [], []
    for hh in heads:
        slope = jnp.concatenate([slope_ref[hh]] * (tq // LANES), axis=1) * LOG2_E
        s_hi = slope.astype(BF16).astype(F32)
        s_lo = slope - s_hi
        feat = jnp.where(r8 == 0, POS_DIGIT_BASE * s_hi,
                         jnp.where(r8 == 1, s_hi,
                                   jnp.where(r8 == 2, POS_DIGIT_BASE * s_lo,
                                             jnp.where(r8 == 3, s_lo, 0.0)))).astype(BF16)
        aug = jnp.concatenate([feat, zero_pad], axis=0)
        q_all = qt_ref[0, hh * ATTN_V_DIM:(hh + 1) * ATTN_V_DIM, :]
        q_aug.append((jnp.concatenate([q_all[0:ATTN_HEAD_DIM], zero_half, aug], axis=0),
                      jnp.concatenate([zero_half, q_all[ATTN_HEAD_DIM:], aug], axis=0)))
        slope_row.append(slope[0:1, :])
    pos = pos_ref[...]

    m_ref[...] = jnp.full(m_ref.shape, -jnp.inf, F32)
    l_ref[...] = jnp.zeros_like(l_ref)
    acc_ref[...] = jnp.zeros_like(acc_ref)

    def scores(ki, slot):
        rows = pl.ds(pl.multiple_of(ki * tk, tk), tk)
        for hh in heads:
            k_aug = jnp.concatenate(
                [k_ref[rows, hh * ATTN_V_DIM:(hh + 1) * ATTN_V_DIM], pos], axis=1)
            for j in range(2):
                s_ref[slot, hh, j] = _dot(k_aug, q_aug[hh][j])

    def softmax_pv(ki, slot, mask):
        for hh in heads:
            v_blk = vt_ref[ki, hh * ATTN_V_DIM:(hh + 1) * ATTN_V_DIM, :]
            c_blk = slope_row[hh] * ((ki - qi) * tk).astype(F32)
            for j in range(2):
                s = s_ref[slot, hh, j]
                if mask is not None:
                    s = jnp.where(mask, s, -jnp.inf)
                m_old = m_ref[hh, j]
                m_new = jnp.maximum(m_old, jnp.max(s, axis=0, keepdims=True) + c_blk)
                p = jnp.exp2(s - (m_new - c_blk))
                alpha = jnp.exp2(m_old - m_new)
                l_ref[hh, j] = alpha * l_ref[hh, j] + jnp.sum(p, axis=0, keepdims=True)
                acc_ref[hh, j] = alpha * acc_ref[hh, j] + _dot(v_blk, p.astype(BF16))
                m_ref[hh, j] = m_new

    kk = lax.broadcasted_iota(jnp.int32, (tk, tq), 0)
    qq = lax.broadcasted_iota(jnp.int32, (tk, tq), 1)
    causal = kk <= qq

    @pl.when(qi == 0)
    def _():
        scores(0, 0)
        softmax_pv(0, 0, causal)

    @pl.when(qi > 0)
    def _():
        scores(0, 0)
        scores(1, 1)
        softmax_pv(0, 0, None)

        def body(i, carry):
            b0 = 2 * i + 1
            scores(b0 + 1, 0)
            softmax_pv(b0, 1, None)
            scores(b0 + 2, 1)
            softmax_pv(b0 + 1, 0, None)
            return carry

        lax.fori_loop(0, (qi - 1) // 2, body, 0)

        @pl.when(qi % 2 == 1)
        def _():
            softmax_pv(qi, 1, causal)

        @pl.when(qi % 2 == 0)
        def _():
            scores(qi, 0)
            softmax_pv(qi - 1, 1, None)
            softmax_pv(qi, 0, causal)

    lv = lam_ref[...]
    lam = (jnp.exp(jnp.sum(lv[0:1] * lv[1:2], axis=-1, keepdims=True))
           - jnp.exp(jnp.sum(lv[2:3] * lv[3:4], axis=-1, keepdims=True)) + lambda_init)
    g_col = jnp.concatenate([g_ref[...]] * (tq // LANES), axis=1)
    for hh in heads:
        o = acc_ref[hh, 0] / l_ref[hh, 0] - lam * (acc_ref[hh, 1] / l_ref[hh, 1])
        o = o * lax.rsqrt(jnp.mean(o * o, axis=0, keepdims=True) + RMS_EPS) * g_col
        o_ref[:, hh * ATTN_V_DIM:(hh + 1) * ATTN_V_DIM] = (
            o * (1.0 - lambda_init)).T.astype(BF16)


def _attention(qt, k, vt, pos, slopes, lam_vecs, g_wide, lambda_init, attn_layer, batch, seq):
    t = k.shape[0]
    tq = qt.shape[2]
    nq = seq // tq
    hps = ATTN_HEADS_PER_STEP
    width = hps * ATTN_V_DIM
    return pl.pallas_call(
        functools.partial(_attn_kernel, lambda_init),
        grid=(batch, ATTN_HEADS // hps, nq),
        in_specs=[
            pl.BlockSpec((1, width, tq), lambda b, h, q: (b * nq + q, h, 0)),
            pl.BlockSpec((seq, width), lambda b, h, q: (b, h)),
            pl.BlockSpec((nq, width, tq), lambda b, h, q: (b, h, 0)),
            _resident((tq, LANES)),
            pl.BlockSpec((hps, SUBLANES, LANES), lambda b, h, q: (h, 0, 0)),
            _layer(attn_layer, (4, ATTN_HEAD_DIM)),
            _layer(attn_layer, (ATTN_V_DIM, LANES)),
        ],
        out_specs=pl.BlockSpec((tq, width), lambda b, h, q: (b * nq + q, h)),
        out_shape=jax.ShapeDtypeStruct((t, D_MODEL), BF16),
        scratch_shapes=[
            pltpu.VMEM((2, hps, 2, tq, tq), F32),
            pltpu.VMEM((hps, 2, 1, tq), F32),
            pltpu.VMEM((hps, 2, 1, tq), F32),
            pltpu.VMEM((hps, 2, ATTN_V_DIM, tq), F32),
        ],
        compiler_params=_params(3),
        name="diff_attention",
    )(qt, k, vt, pos, slopes, lam_vecs, g_wide)


def _ffn_kernel(tiles_per_seq, fc, y_ref, wo_ref, gmix_ref, x_ref, gpre_ref, wup_ref, cw_ref, cb_ref,
                wdn_ref, gpost_ref, o_ref, act_ref, carry_ref):
    @pl.when(pl.program_id(0) % tiles_per_seq == 0)
    def _():
        carry_ref[...] = jnp.zeros_like(carry_ref)

    x = x_ref[...] + _rms(_dot(y_ref[...], wo_ref[...]), gmix_ref[...])
    hn = _rms(x, gpre_ref[...]).astype(BF16)
    for c in range(D_FF // fc):
        halves = []
        for part in range(2):
            cols = slice(part * D_FF + c * fc, part * D_FF + (c + 1) * fc)
            h = _dot(hn, wup_ref[:, cols])
            halves.append(_causal_conv(h, carry_ref, cw_ref, cb_ref, cols, FFN_CONV, horner=False))
        act_ref[:, c * fc:(c + 1) * fc] = (_silu(halves[0]) * halves[1]).astype(BF16)
    f = _dot(act_ref[...], wdn_ref[...])
    o_ref[...] = x + _rms(f, gpost_ref[...])


def _mixer_out_ffn(y, wo, gmix, x, gpre, wup, conv_w, conv_b, wdn, gpost, layer, mix_layer, seq):
    t, k = y.shape
    tm = min(ROW_TILE, seq)
    fc = FFN_CHUNK
    row = lambda i: (i, 0)
    return pl.pallas_call(
        functools.partial(_ffn_kernel, seq // tm, fc),
        grid=(t // tm,),
        in_specs=[
            pl.BlockSpec((tm, k), row),
            _layer(mix_layer, (k, D_MODEL)),
            _layer(layer, (1, D_MODEL)),
            pl.BlockSpec((tm, D_MODEL), row),
            _layer(layer, (1, D_MODEL)),
            _layer(layer, (D_MODEL, 2 * D_FF)),
            _layer(layer, (FFN_CONV, 2 * D_FF)),
            _layer(layer, (1, 2 * D_FF)),
            _layer(layer, (D_FF, D_MODEL)),
            _layer(layer, (1, D_MODEL)),
        ],
        out_specs=pl.BlockSpec((tm, D_MODEL), row),
        out_shape=jax.ShapeDtypeStruct((t, D_MODEL), F32),
        scratch_shapes=[
            pltpu.VMEM((tm, D_FF), BF16),
            pltpu.VMEM((SUBLANES, 2 * D_FF), F32),
        ],
        compiler_params=_params(1),
        name="mixer_out_ffn",
    )(y, wo, gmix, x, gpre, wup, conv_w, conv_b, wdn, gpost)


def _rows(v):
    return v.astype(F32)[:, None, :]


def _head_expand_matrix():
    head_of_lane = jnp.arange(D_INNER) // SSM_HEAD_DIM
    return (jnp.arange(SSM_HEADS)[:, None] == head_of_lane[None, :]).astype(BF16)


def _alibi_features(tk):
    kk = jnp.arange(tk)
    base = int(POS_DIGIT_BASE)
    cols = jnp.stack([kk // base, kk % base, kk // base, kk % base], axis=1).astype(F32)
    return jnp.pad(cols, ((0, 0), (0, LANES - 4))).astype(BF16)


def kernel(x, ssm_w_in, ssm_conv_w, ssm_conv_b, ssm_dt_bias, ssm_A_log, ssm_D, ssm_norm_g, ssm_w_out, attn_w_qkv, attn_lambda_q1, attn_lambda_k1, attn_lambda_q2, attn_lambda_k2, attn_subln_g, attn_w_out, norm_mix_pre, norm_mix_post, norm_ffn_pre, norm_ffn_post, ffn_w_up, ffn_conv_w, ffn_conv_b, ffn_w_down):
    batch, seq, d_model = x.shape
    depth = norm_mix_pre.shape[0]
    assert d_model == D_MODEL and seq % ROW_TILE == 0 and seq % ATTN_TILE == 0
    assert seq % min(INPROJ_ROW_TILE, seq) == 0 and seq % (SSD_CHUNKS_PER_STEP * SSM_CHUNK) == 0
    t = batch * seq
    xs = x.reshape(t, D_MODEL).astype(F32)

    w_zx = ssm_w_in[:, :, :D_INNER + SSM_CONV_DIM].astype(BF16)
    w_dt = jnp.pad(ssm_w_in[:, :, D_INNER + SSM_CONV_DIM:],
                   ((0, 0), (0, 0), (0, LANES - SSM_HEADS))).astype(BF16)
    conv_w, conv_b = ssm_conv_w.astype(F32), _rows(ssm_conv_b)
    dt_bias, a_log = _rows(ssm_dt_bias), _rows(ssm_A_log)
    d_wide = _rows(jnp.repeat(ssm_D, SSM_HEAD_DIM, axis=1))
    ssm_g = _rows(ssm_norm_g)
    w_out_ssm = ssm_w_out.astype(BF16)
    w_qkv = attn_w_qkv.astype(BF16)
    lam_vecs = jnp.stack([attn_lambda_q1, attn_lambda_k1, attn_lambda_q2, attn_lambda_k2],
                         axis=1).astype(F32)
    g_wide = jnp.broadcast_to(attn_subln_g.astype(F32)[:, :, None],
                              attn_subln_g.shape + (LANES,))
    w_out_attn = attn_w_out.astype(BF16)
    g_mix_pre, g_mix_post = _rows(norm_mix_pre), _rows(norm_mix_post)
    g_ffn_pre, g_ffn_post = _rows(norm_ffn_pre), _rows(norm_ffn_post)
    w_up, w_down = ffn_w_up.astype(BF16), ffn_w_down.astype(BF16)
    f_conv_w, f_conv_b = ffn_conv_w.astype(F32), _rows(ffn_conv_b)

    expand = _head_expand_matrix()
    pos = _alibi_features(min(ATTN_TILE, seq))
    slopes = jnp.exp2(-8.0 * jnp.arange(1, ATTN_HEADS + 1, dtype=F32) / ATTN_HEADS)
    slopes = jnp.broadcast_to(slopes[:, None, None], (ATTN_HEADS, SUBLANES, LANES))

    for i in range(depth):
        j = i // 2
        if i % 2 == 0:
            z, xbc, dt = _ssm_inproj(xs, g_mix_pre, w_zx, w_dt, conv_w, conv_b, i, j, seq)
            mixed = _ssd(xbc, z, dt, dt_bias, a_log, d_wide, ssm_g, expand, j, batch, seq)
            w_o = w_out_ssm
        else:
            lambda_init = 0.8 - 0.6 * math.exp(-0.3 * i)
            qt, k, vt = _qkv(xs, g_mix_pre, w_qkv, i, j, seq)
            mixed = _attention(qt, k, vt, pos, slopes, lam_vecs, g_wide, lambda_init, j, batch, seq)
            w_o = w_out_attn
        xs = _mixer_out_ffn(mixed, w_o, g_mix_post, xs, g_ffn_pre, w_up, f_conv_w, f_conv_b,
                            w_down, g_ffn_post, i, j, seq)
    return xs.reshape(batch, seq, D_MODEL).astype(x.dtype)
```

```python
import functools
import math

import jax
import jax.numpy as jnp
from jax import lax
from jax.experimental import pallas as pl
from jax.experimental.pallas import tpu as pltpu

F32 = jnp.float32
BF16 = jnp.bfloat16

D_MODEL = 1024
D_INNER = 2048
SSM_HEAD_DIM = 64
SSM_HEADS = 32
SSM_GROUPS = 8
SSM_STATE = 128
SSM_CONV = 4
SSM_CHUNK = 128
SSM_BC_DIM = SSM_GROUPS * SSM_STATE
SSM_CONV_DIM = D_INNER + 2 * SSM_BC_DIM
SSM_IN_DIM = D_INNER + SSM_CONV_DIM + SSM_HEADS
SSM_IN_PAD = D_INNER + SSM_CONV_DIM + 128
HEADS_PER_GROUP = SSM_HEADS // SSM_GROUPS
GROUP_WIDTH = D_INNER // SSM_GROUPS
ATTN_HEAD_DIM = 64
ATTN_HEADS = 8
ATTN_V_DIM = 128
D_FF = 2816
FFN_CONV = 3
RMS_EPS = 1e-6
LOG2_E = math.log2(math.e)
POS_DIGIT_BASE = 16.0

LANES = 128
SUBLANES = 8
V7X_VMEM_LIMIT_BYTES = 56 * 1024 * 1024

ROW_TILE = 512
INPROJ_ROW_TILE = 1024
ATTN_TILE = 512
N_CHUNK = 512
FFN_CHUNK = 256
SSD_CHUNKS_PER_STEP = 4
ATTN_HEADS_PER_STEP = 4


def _params(n_axes):
    return pltpu.CompilerParams(
        dimension_semantics=("arbitrary",) * n_axes,
        vmem_limit_bytes=V7X_VMEM_LIMIT_BYTES,
    )


def _resident(shape):
    zeros = (0,) * len(shape)
    return pl.BlockSpec(shape, lambda *_: zeros, pipeline_mode=pl.Buffered(1))


def _layer(index, shape):
    idx = (index,) + (0,) * len(shape)
    return pl.BlockSpec((None,) + tuple(shape), lambda *_: idx, pipeline_mode=pl.Buffered(1))


def _rms(x, g):
    return x * lax.rsqrt(jnp.mean(x * x, axis=-1, keepdims=True) + RMS_EPS) * g


def _silu(x):
    half = 0.5 * x
    return half + half * jnp.tanh(half)


def _dot(a, b):
    return jnp.dot(a, b, preferred_element_type=F32)


def _dot_nt(a, b):
    return lax.dot_general(a, b, (((1,), (1,)), ((), ())), preferred_element_type=F32)


def _causal_conv(h, carry_ref, w_ref, b_ref, cols, width, horner):
    rows, c = h.shape
    n = rows // SUBLANES + 1
    prev = carry_ref[:, cols]
    carry_ref[:, cols] = h[rows - SUBLANES:rows, :]
    tiles = jnp.concatenate([prev, h], axis=0).reshape(n, SUBLANES, c)
    sub = lax.broadcasted_iota(jnp.int32, (n, SUBLANES, c), 1)

    def shift(a, k):
        rot = pltpu.roll(a, k, 1)
        above = jnp.concatenate([rot[:1], rot[:-1]], axis=0)
        return jnp.where(sub < k, above, rot)

    def tap(k):
        return tiles * w_ref[k:k + 1, cols].reshape(1, 1, c)

    if horner:
        a = tap(0)
        for k in range(1, width):
            a = shift(a, 1) + tap(k)
    else:
        a = tap(width - 1)
        for k in range(1, width):
            a = a + shift(tiles, k) * w_ref[width - 1 - k:width - k, cols].reshape(1, 1, c)
    return a[1:].reshape(rows, c) + b_ref[:, cols]


def _ssm_inproj_kernel(tiles_per_seq, x_ref, g_ref, w_ref, cw_ref, cb_ref,
                       z_ref, xbc_ref, dt_ref, carry_ref):
    @pl.when(pl.program_id(0) % tiles_per_seq == 0)
    def _():
        carry_ref[...] = jnp.zeros_like(carry_ref)

    hn = _rms(x_ref[...], g_ref[...]).astype(BF16)
    dt_ref[...] = _dot(hn, w_ref[:, D_INNER + SSM_CONV_DIM:])
    n_chunks = SSM_CONV_DIM // N_CHUNK
    zc = D_INNER // n_chunks
    for c in range(n_chunks):
        cols = slice(c * N_CHUNK, (c + 1) * N_CHUNK)
        h = _dot(hn, w_ref[:, D_INNER + c * N_CHUNK:D_INNER + (c + 1) * N_CHUNK])
        zcols = slice(c * zc, (c + 1) * zc)
        z_ref[:, zcols] = _dot(hn, w_ref[:, zcols]).astype(BF16)
        y = _causal_conv(h, carry_ref, cw_ref, cb_ref, cols, SSM_CONV, horner=True)
        xbc_ref[:, cols] = _silu(y).astype(BF16)


def _ssm_inproj(x, g, w_in, conv_w, conv_b, layer, ssm_layer, seq):
    t = x.shape[0]
    tm = min(INPROJ_ROW_TILE, seq)
    row = lambda i: (i, 0)
    return pl.pallas_call(
        functools.partial(_ssm_inproj_kernel, seq // tm),
        grid=(t // tm,),
        in_specs=[
            pl.BlockSpec((tm, D_MODEL), row),
            _layer(layer, (1, D_MODEL)),
            _layer(ssm_layer, (D_MODEL, SSM_IN_PAD)),
            _layer(ssm_layer, (SSM_CONV, SSM_CONV_DIM)),
            _layer(ssm_layer, (1, SSM_CONV_DIM)),
        ],
        out_specs=[
            pl.BlockSpec((tm, D_INNER), row),
            pl.BlockSpec((tm, SSM_CONV_DIM), row),
            pl.BlockSpec((tm, LANES), row),
        ],
        out_shape=[
            jax.ShapeDtypeStruct((t, D_INNER), BF16),
            jax.ShapeDtypeStruct((t, SSM_CONV_DIM), BF16),
            jax.ShapeDtypeStruct((t, LANES), F32),
        ],
        scratch_shapes=[pltpu.VMEM((SUBLANES, SSM_CONV_DIM), F32)],
        compiler_params=_params(1),
        name="ssm_inproj",
    )(x, g, w_in, conv_w, conv_b)


def _ssd_chunk(rows, xbc_ref, z_ref, dt_ref, dtb_ref, alog_ref, dexp_ref, ng_ref, e_ref,
               y_ref, state_ref):
    L = SSM_CHUNK
    H = SSM_HEADS

    dt_raw = dt_ref[rows, 0:H] + dtb_ref[...]
    dt = jnp.maximum(dt_raw, 0.0) + jnp.log(1.0 + jnp.exp(-jnp.abs(dt_raw)))
    acs = dt * (-jnp.exp(alog_ref[...]) * LOG2_E)
    row_h = lax.broadcasted_iota(jnp.int32, (L, H), 0)
    k = 1
    while k < L:
        acs = acs + jnp.where(row_h >= k, pltpu.roll(acs, k, 0), 0.0)
        k *= 2
    acs_t = acs.T
    rowp_t = acs_t - jnp.log2(dt.T)
    w_t = jnp.exp2(acs_t[:, L - 1:L] - rowp_t)

    ea = jnp.exp2(acs)
    ea_hi = ea.astype(BF16)
    ea_lo = (ea - ea_hi.astype(F32)).astype(BF16)
    ea_wide = _dot(ea_hi, e_ref[...]) + _dot(ea_lo, e_ref[...])
    cd_wide = ea_wide[L - 1:L, :]

    row = lax.broadcasted_iota(jnp.int32, (L, LANES), 0)
    lane = lax.broadcasted_iota(jnp.int32, (L, LANES), 1)
    causal = row >= lane
    keep_l = jnp.where(lane < SSM_HEAD_DIM, 1.0, 0.0).astype(BF16)
    keep_r = jnp.where(lane < SSM_HEAD_DIM, 0.0, 1.0).astype(BF16)

    for g in range(SSM_GROUPS):
        b_g = xbc_ref[rows, D_INNER + g * SSM_STATE:D_INNER + (g + 1) * SSM_STATE]
        c_g = xbc_ref[rows, D_INNER + SSM_BC_DIM + g * SSM_STATE:
                      D_INNER + SSM_BC_DIM + (g + 1) * SSM_STATE]
        cb = _dot_nt(c_g, b_g)
        b_t = b_g.astype(F32).T
        gcols = slice(g * GROUP_WIDTH, (g + 1) * GROUP_WIDTH)
        y_off = _dot(c_g, state_ref[g].astype(BF16)) * ea_wide[:, gcols]
        ys = []
        for pr in range(HEADS_PER_GROUP // 2):
            heads = (g * HEADS_PER_GROUP + 2 * pr, g * HEADS_PER_GROUP + 2 * pr + 1)
            cols = slice((g * 2 + pr) * LANES, (g * 2 + pr + 1) * LANES)
            scols = slice(pr * LANES, (pr + 1) * LANES)
            xp = xbc_ref[rows, cols]
            sp = state_ref[g, :, scols]
            m_parts, bw_parts = [], []
            for h in heads:
                a_col = jnp.broadcast_to(acs[:, h:h + 1], (L, L))
                a_row = jnp.broadcast_to(rowp_t[h:h + 1, :], (L, L))
                m_parts.append(
                    (cb * jnp.exp2(jnp.where(causal, a_col - a_row, -jnp.inf))).astype(BF16))
                bw_parts.append((b_t * jnp.broadcast_to(w_t[h:h + 1, :], (L, L))).astype(BF16))
            x_blk = jnp.concatenate([xp * keep_l, xp * keep_r], axis=0)
            y_p = _dot(jnp.concatenate(m_parts, axis=1), x_blk) + y_off[:, scols]
            y_p = y_p + xp.astype(F32) * dexp_ref[:, cols]
            state_ref[g, :, scols] = sp * cd_wide[:, cols] + _dot(
                jnp.concatenate(bw_parts, axis=1), x_blk)
            ys.append(y_p)
        y_g = jnp.concatenate(ys, axis=1) * _silu(z_ref[rows, gcols].astype(F32))
        y_ref[rows, gcols] = _rms(y_g, ng_ref[:, gcols]).astype(BF16)


def _ssd_kernel(*refs):
    state_ref = refs[-1]

    @pl.when(pl.program_id(1) == 0)
    def _():
        state_ref[...] = jnp.zeros_like(state_ref)

    for sub in range(SSD_CHUNKS_PER_STEP):
        _ssd_chunk(slice(sub * SSM_CHUNK, (sub + 1) * SSM_CHUNK), *refs)


def _ssd(xbc, z, dt, dt_bias, a_log, d_wide, norm_g, expand, ssm_layer, batch, seq):
    t = xbc.shape[0]
    rows = SSD_CHUNKS_PER_STEP * SSM_CHUNK
    n_steps = seq // rows
    row = lambda b, c: (b * n_steps + c, 0)
    return pl.pallas_call(
        _ssd_kernel,
        grid=(batch, n_steps),
        in_specs=[
            pl.BlockSpec((rows, SSM_CONV_DIM), row),
            pl.BlockSpec((rows, D_INNER), row),
            pl.BlockSpec((rows, LANES), row),
            _layer(ssm_layer, (1, SSM_HEADS)),
            _layer(ssm_layer, (1, SSM_HEADS)),
            _layer(ssm_layer, (1, D_INNER)),
            _layer(ssm_layer, (1, D_INNER)),
            _resident((SSM_HEADS, D_INNER)),
        ],
        out_specs=pl.BlockSpec((rows, D_INNER), row),
        out_shape=jax.ShapeDtypeStruct((t, D_INNER), BF16),
        scratch_shapes=[pltpu.VMEM((SSM_GROUPS, SSM_STATE, GROUP_WIDTH), F32)],
        compiler_params=_params(2),
        name="ssd_scan",
    )(xbc, z, dt, dt_bias, a_log, d_wide, norm_g, expand)


def _qkv_kernel(x_ref, g_ref, w_ref, qt_ref, k_ref, vt_ref, wqt_ref, wvt_ref):
    @pl.when(pl.program_id(0) == 0)
    def _():
        wqt_ref[...] = w_ref[:, 0:D_MODEL].T
        wvt_ref[...] = w_ref[:, 2 * D_MODEL:3 * D_MODEL].T

    hn = _rms(x_ref[...], g_ref[...]).astype(BF16)
    scale = ATTN_HEAD_DIM ** -0.5 * LOG2_E
    for c in range(D_MODEL // N_CHUNK):
        rows = slice(c * N_CHUNK, (c + 1) * N_CHUNK)
        qt_ref[0, rows, :] = (_dot_nt(wqt_ref[rows, :], hn) * scale).astype(BF16)
        vt_ref[0, rows, :] = _dot_nt(wvt_ref[rows, :], hn).astype(BF16)
        k_ref[:, rows] = _dot(hn, w_ref[:, D_MODEL + c * N_CHUNK:D_MODEL + (c + 1) * N_CHUNK]
                              ).astype(BF16)


def _qkv(x, g, w_qkv, layer, attn_layer, seq):
    t = x.shape[0]
    tm = min(ATTN_TILE, seq)
    row = lambda i: (i, 0)
    tile = lambda i: (i, 0, 0)
    return pl.pallas_call(
        _qkv_kernel,
        grid=(t // tm,),
        in_specs=[
            pl.BlockSpec((tm, D_MODEL), row),
            _layer(layer, (1, D_MODEL)),
            _layer(attn_layer, (D_MODEL, 3 * D_MODEL)),
        ],
        out_specs=[
            pl.BlockSpec((1, D_MODEL, tm), tile),
            pl.BlockSpec((tm, D_MODEL), row),
            pl.BlockSpec((1, D_MODEL, tm), tile),
        ],
        out_shape=[
            jax.ShapeDtypeStruct((t // tm, D_MODEL, tm), BF16),
            jax.ShapeDtypeStruct((t, D_MODEL), BF16),
            jax.ShapeDtypeStruct((t // tm, D_MODEL, tm), BF16),
        ],
        scratch_shapes=[
            pltpu.VMEM((D_MODEL, D_MODEL), BF16),
            pltpu.VMEM((D_MODEL, D_MODEL), BF16),
        ],
        compiler_params=_params(1),
        name="attn_qkv",
    )(x, g, w_qkv)


def _attn_kernel(lambda_init, qt_ref, k_ref, vt_ref, pos_ref, slope_ref, lam_ref, g_ref,
                 o_ref, s_ref, m_ref, l_ref, acc_ref):
    tq = qt_ref.shape[2]
    tk = tq
    qi = pl.program_id(2)
    heads = range(ATTN_HEADS_PER_STEP)

    r8 = lax.broadcasted_iota(jnp.int32, (SUBLANES, tq), 0)
    zero_half = jnp.zeros((ATTN_HEAD_DIM, tq), BF16)
    zero_pad = jnp.zeros((LANES - SUBLANES, tq), BF16)
    q_aug, slope_row = [], []
    for hh in heads:
        slope = jnp.concatenate([slope_ref[hh]] * (tq // LANES), axis=1) * LOG2_E
        s_hi = slope.astype(BF16).astype(F32)
        s_lo = slope - s_hi
        feat = jnp.where(r8 == 0, POS_DIGIT_BASE * s_hi,
                         jnp.where(r8 == 1, s_hi,
                                   jnp.where(r8 == 2, POS_DIGIT_BASE * s_lo,
                                             jnp.where(r8 == 3, s_lo, 0.0)))).astype(BF16)
        aug = jnp.concatenate([feat, zero_pad], axis=0)
        q_all = qt_ref[0, hh * ATTN_V_DIM:(hh + 1) * ATTN_V_DIM, :]
        q_aug.append((jnp.concatenate([q_all[0:ATTN_HEAD_DIM], zero_half, aug], axis=0),
                      jnp.concatenate([zero_half, q_all[ATTN_HEAD_DIM:], aug], axis=0)))
        slope_row.append(slope[0:1, :])
    pos = pos_ref[...]

    m_ref[...] = jnp.full(m_ref.shape, -jnp.inf, F32)
    l_ref[...] = jnp.zeros_like(l_ref)
    acc_ref[...] = jnp.zeros_like(acc_ref)

    def scores(ki, slot):
        rows = pl.ds(pl.multiple_of(ki * tk, tk), tk)
        for hh in heads:
            k_aug = jnp.concatenate(
                [k_ref[rows, hh * ATTN_V_DIM:(hh + 1) * ATTN_V_DIM], pos], axis=1)
            for j in range(2):
                s_ref[slot, hh, j] = _dot(k_aug, q_aug[hh][j])

    def softmax_pv(ki, slot, mask):
        for hh in heads:
            v_blk = vt_ref[ki, hh * ATTN_V_DIM:(hh + 1) * ATTN_V_DIM, :]
            c_blk = slope_row[hh] * ((ki - qi) * tk).astype(F32)
            for j in range(2):
                s = s_ref[slot, hh, j]
                if mask is not None:
                    s = jnp.where(mask, s, -jnp.inf)
                m_old = m_ref[hh, j]
                m_new = jnp.maximum(m_old, jnp.max(s, axis=0, keepdims=True) + c_blk)
                p = jnp.exp2(s - (m_new - c_blk))
                alpha = jnp.exp2(m_old - m_new)
                l_ref[hh, j] = alpha * l_ref[hh, j] + jnp.sum(p, axis=0, keepdims=True)
                acc_ref[hh, j] = alpha * acc_ref[hh, j] + _dot(v_blk, p.astype(BF16))
                m_ref[hh, j] = m_new

    kk = lax.broadcasted_iota(jnp.int32, (tk, tq), 0)
    qq = lax.broadcasted_iota(jnp.int32, (tk, tq), 1)
    causal = kk <= qq

    @pl.when(qi == 0)
    def _():
        scores(0, 0)
        softmax_pv(0, 0, causal)

    @pl.when(qi > 0)
    def _():
        scores(0, 0)
        scores(1, 1)
        softmax_pv(0, 0, None)

        def body(i, carry):
            b0 = 2 * i + 1
            scores(b0 + 1, 0)
            softmax_pv(b0, 1, None)
            scores(b0 + 2, 1)
            softmax_pv(b0 + 1, 0, None)
            return carry

        lax.fori_loop(0, (qi - 1) // 2, body, 0)

        @pl.when(qi % 2 == 1)
        def _():
            softmax_pv(qi, 1, causal)

        @pl.when(qi % 2 == 0)
        def _():
            scores(qi, 0)
            softmax_pv(qi - 1, 1, None)
            softmax_pv(qi, 0, causal)

    lv = lam_ref[...]
    lam = (jnp.exp(jnp.sum(lv[0:1] * lv[1:2], axis=-1, keepdims=True))
           - jnp.exp(jnp.sum(lv[2:3] * lv[3:4], axis=-1, keepdims=True)) + lambda_init)
    g_col = jnp.concatenate([g_ref[...]] * (tq // LANES), axis=1)
    for hh in heads:
        o = acc_ref[hh, 0] / l_ref[hh, 0] - lam * (acc_ref[hh, 1] / l_ref[hh, 1])
        o = o * lax.rsqrt(jnp.mean(o * o, axis=0, keepdims=True) + RMS_EPS) * g_col
        o_ref[:, hh * ATTN_V_DIM:(hh + 1) * ATTN_V_DIM] = (
            o * (1.0 - lambda_init)).T.astype(BF16)


def _attention(qt, k, vt, pos, slopes, lam_vecs, g_wide, lambda_init, attn_layer, batch, seq):
    t = k.shape[0]
    tq = qt.shape[2]
    nq = seq // tq
    hps = ATTN_HEADS_PER_STEP
    width = hps * ATTN_V_DIM
    return pl.pallas_call(
        functools.partial(_attn_kernel, lambda_init),
        grid=(batch, ATTN_HEADS // hps, nq),
        in_specs=[
            pl.BlockSpec((1, width, tq), lambda b, h, q: (b * nq + q, h, 0)),
            pl.BlockSpec((seq, width), lambda b, h, q: (b, h)),
            pl.BlockSpec((nq, width, tq), lambda b, h, q: (b, h, 0)),
            _resident((tq, LANES)),
            pl.BlockSpec((hps, SUBLANES, LANES), lambda b, h, q: (h, 0, 0)),
            _layer(attn_layer, (4, ATTN_HEAD_DIM)),
            _layer(attn_layer, (ATTN_V_DIM, LANES)),
        ],
        out_specs=pl.BlockSpec((tq, width), lambda b, h, q: (b * nq + q, h)),
        out_shape=jax.ShapeDtypeStruct((t, D_MODEL), BF16),
        scratch_shapes=[
            pltpu.VMEM((2, hps, 2, tq, tq), F32),
            pltpu.VMEM((hps, 2, 1, tq), F32),
            pltpu.VMEM((hps, 2, 1, tq), F32),
            pltpu.VMEM((hps, 2, ATTN_V_DIM, tq), F32),
        ],
        compiler_params=_params(3),
        name="diff_attention",
    )(qt, k, vt, pos, slopes, lam_vecs, g_wide)


def _ffn_kernel(tiles_per_seq, fc, y_ref, wo_ref, gmix_ref, x_ref, gpre_ref, wup_ref, cw_ref, cb_ref,
                wdn_ref, gpost_ref, o_ref, act_ref, carry_ref):
    @pl.when(pl.program_id(0) % tiles_per_seq == 0)
    def _():
        carry_ref[...] = jnp.zeros_like(carry_ref)

    x = x_ref[...] + _rms(_dot(y_ref[...], wo_ref[...]), gmix_ref[...])
    hn = _rms(x, gpre_ref[...]).astype(BF16)
    for c in range(D_FF // fc):
        halves = []
        for part in range(2):
            cols = slice(part * D_FF + c * fc, part * D_FF + (c + 1) * fc)
            h = _dot(hn, wup_ref[:, cols])
            halves.append(_causal_conv(h, carry_ref, cw_ref, cb_ref, cols, FFN_CONV, horner=False))
        act_ref[:, c * fc:(c + 1) * fc] = (_silu(halves[0]) * halves[1]).astype(BF16)
    f = _dot(act_ref[...], wdn_ref[...])
    o_ref[...] = x + _rms(f, gpost_ref[...])


def _mixer_out_ffn(y, wo, gmix, x, gpre, wup, conv_w, conv_b, wdn, gpost, layer, mix_layer, seq):
    t, k = y.shape
    tm = min(ROW_TILE, seq)
    fc = FFN_CHUNK
    row = lambda i: (i, 0)
    return pl.pallas_call(
        functools.partial(_ffn_kernel, seq // tm, fc),
        grid=(t // tm,),
        in_specs=[
            pl.BlockSpec((tm, k), row),
            _layer(mix_layer, (k, D_MODEL)),
            _layer(layer, (1, D_MODEL)),
            pl.BlockSpec((tm, D_MODEL), row),
            _layer(layer, (1, D_MODEL)),
            _layer(layer, (D_MODEL, 2 * D_FF)),
            _layer(layer, (FFN_CONV, 2 * D_FF)),
            _layer(layer, (1, 2 * D_FF)),
            _layer(layer, (D_FF, D_MODEL)),
            _layer(layer, (1, D_MODEL)),
        ],
        out_specs=pl.BlockSpec((tm, D_MODEL), row),
        out_shape=jax.ShapeDtypeStruct((t, D_MODEL), F32),
        scratch_shapes=[
            pltpu.VMEM((tm, D_FF), BF16),
            pltpu.VMEM((SUBLANES, 2 * D_FF), F32),
        ],
        compiler_params=_params(1),
        name="mixer_out_ffn",
    )(y, wo, gmix, x, gpre, wup, conv_w, conv_b, wdn, gpost)


def _rows(v):
    return v.astype(F32)[:, None, :]


def _head_expand_matrix():
    head_of_lane = jnp.arange(D_INNER) // SSM_HEAD_DIM
    return (jnp.arange(SSM_HEADS)[:, None] == head_of_lane[None, :]).astype(BF16)


def _alibi_features(tk):
    kk = jnp.arange(tk)
    base = int(POS_DIGIT_BASE)
    cols = jnp.stack([kk // base, kk % base, kk // base, kk % base], axis=1).astype(F32)
    return jnp.pad(cols, ((0, 0), (0, LANES - 4))).astype(BF16)


def kernel(x, ssm_w_in, ssm_conv_w, ssm_conv_b, ssm_dt_bias, ssm_A_log, ssm_D, ssm_norm_g, ssm_w_out, attn_w_qkv, attn_lambda_q1, attn_lambda_k1, attn_lambda_q2, attn_lambda_k2, attn_subln_g, attn_w_out, norm_mix_pre, norm_mix_post, norm_ffn_pre, norm_ffn_post, ffn_w_up, ffn_conv_w, ffn_conv_b, ffn_w_down):
    batch, seq, d_model = x.shape
    depth = norm_mix_pre.shape[0]
    assert d_model == D_MODEL and seq % ROW_TILE == 0 and seq % ATTN_TILE == 0
    assert seq % min(INPROJ_ROW_TILE, seq) == 0 and seq % (SSD_CHUNKS_PER_STEP * SSM_CHUNK) == 0
    t = batch * seq
    xs = x.reshape(t, D_MODEL).astype(F32)

    w_in = jnp.pad(ssm_w_in, ((0, 0), (0, 0), (0, SSM_IN_PAD - SSM_IN_DIM))).astype(BF16)
    conv_w, conv_b = ssm_conv_w.astype(F32), _rows(ssm_conv_b)
    dt_bias, a_log = _rows(ssm_dt_bias), _rows(ssm_A_log)
    d_wide = _rows(jnp.repeat(ssm_D, SSM_HEAD_DIM, axis=1))
    ssm_g = _rows(ssm_norm_g)
    w_out_ssm = ssm_w_out.astype(BF16)
    w_qkv = attn_w_qkv.astype(BF16)
    lam_vecs = jnp.stack([attn_lambda_q1, attn_lambda_k1, attn_lambda_q2, attn_lambda_k2],
                         axis=1).astype(F32)
    g_wide = jnp.broadcast_to(attn_subln_g.astype(F32)[:, :, None],
                              attn_subln_g.shape + (LANES,))
    w_out_attn = attn_w_out.astype(BF16)
    g_mix_pre, g_mix_post = _rows(norm_mix_pre), _rows(norm_mix_post)
    g_ffn_pre, g_ffn_post = _rows(norm_ffn_pre), _rows(norm_ffn_post)
    w_up, w_down = ffn_w_up.astype(BF16), ffn_w_down.astype(BF16)
    f_conv_w, f_conv_b = ffn_conv_w.astype(F32), _rows(ffn_conv_b)

    expand = _head_expand_matrix()
    pos = _alibi_features(min(ATTN_TILE, seq))
    slopes = jnp.exp2(-8.0 * jnp.arange(1, ATTN_HEADS + 1, dtype=F32) / ATTN_HEADS)
    slopes = jnp.broadcast_to(slopes[:, None, None], (ATTN_HEADS, SUBLANES, LANES))

    for i in range(depth):
        j = i // 2
        if i % 2 == 0:
            z, xbc, dt = _ssm_inproj(xs, g_mix_pre, w_in, conv_w, conv_b, i, j, seq)
            mixed = _ssd(xbc, z, dt, dt_bias, a_log, d_wide, ssm_g, expand, j, batch, seq)
            w_o = w_out_ssm
        else:
            lambda_init = 0.8 - 0.6 * math.exp(-0.3 * i)
            qt, k, vt = _qkv(xs, g_mix_pre, w_qkv, i, j, seq)
            mixed = _attention(qt, k, vt, pos, slopes, lam_vecs, g_wide, lambda_init, j, batch, seq)
            w_o = w_out_attn
        xs = _mixer_out_ffn(mixed, w_o, g_mix_post, xs, g_ffn_pre, w_up, f_conv_w, f_conv_b,
                            w_down, g_ffn_post, i, j, seq)
    return xs.reshape(batch, seq, D_MODEL).astype(x.dtype)
```

```python
import functools
import math

import jax
import jax.numpy as jnp
from jax import lax
from jax.experimental import pallas as pl
from jax.experimental.pallas import tpu as pltpu

F32 = jnp.float32
BF16 = jnp.bfloat16

D_MODEL = 1024
D_INNER = 2048
SSM_HEAD_DIM = 64
SSM_HEADS = 32
SSM_GROUPS = 8
SSM_STATE = 128
SSM_CONV = 4
SSM_CHUNK = 128
SSM_BC_DIM = SSM_GROUPS * SSM_STATE
SSM_CONV_DIM = D_INNER + 2 * SSM_BC_DIM
SSM_IN_DIM = D_INNER + SSM_CONV_DIM + SSM_HEADS
SSM_IN_PAD = D_INNER + SSM_CONV_DIM + 128
HEADS_PER_GROUP = SSM_HEADS // SSM_GROUPS
GROUP_WIDTH = D_INNER // SSM_GROUPS
ATTN_HEAD_DIM = 64
ATTN_HEADS = 8
ATTN_V_DIM = 128
ATTN_SUM_ROWS = 16
D_FF = 2816
FFN_CONV = 3
RMS_EPS = 1e-6
LOG2_E = math.log2(math.e)
POS_DIGIT_BASE = 16.0

LANES = 128
SUBLANES = 8
V7X_VMEM_LIMIT_BYTES = 56 * 1024 * 1024

ROW_TILE = 512
INPROJ_ROW_TILE = 1024
ATTN_TILE = 512
N_CHUNK = 512
FFN_CHUNK = 256
SSD_CHUNKS_PER_STEP = 4
ATTN_HEADS_PER_STEP = 4


def _params(n_axes):
    return pltpu.CompilerParams(
        dimension_semantics=("arbitrary",) * n_axes,
        vmem_limit_bytes=V7X_VMEM_LIMIT_BYTES,
    )


def _resident(shape):
    zeros = (0,) * len(shape)
    return pl.BlockSpec(shape, lambda *_: zeros, pipeline_mode=pl.Buffered(1))


def _layer(index, shape):
    idx = (index,) + (0,) * len(shape)
    return pl.BlockSpec((None,) + tuple(shape), lambda *_: idx, pipeline_mode=pl.Buffered(1))


def _rms(x, g):
    return x * lax.rsqrt(jnp.mean(x * x, axis=-1, keepdims=True) + RMS_EPS) * g


def _silu(x):
    half = 0.5 * x
    return half + half * jnp.tanh(half)


def _dot(a, b):
    return jnp.dot(a, b, preferred_element_type=F32)


def _dot_nt(a, b):
    return lax.dot_general(a, b, (((1,), (1,)), ((), ())), preferred_element_type=F32)


def _causal_conv(h, carry_ref, w_ref, b_ref, cols, width, horner):
    rows, c = h.shape
    n = rows // SUBLANES + 1
    prev = carry_ref[:, cols]
    carry_ref[:, cols] = h[rows - SUBLANES:rows, :]
    tiles = jnp.concatenate([prev, h], axis=0).reshape(n, SUBLANES, c)
    sub = lax.broadcasted_iota(jnp.int32, (n, SUBLANES, c), 1)

    def shift(a, k):
        rot = pltpu.roll(a, k, 1)
        above = jnp.concatenate([rot[:1], rot[:-1]], axis=0)
        return jnp.where(sub < k, above, rot)

    def tap(k):
        return tiles * w_ref[k:k + 1, cols].reshape(1, 1, c)

    if horner:
        a = tap(0)
        for k in range(1, width):
            a = shift(a, 1) + tap(k)
    else:
        a = tap(width - 1)
        for k in range(1, width):
            a = a + shift(tiles, k) * w_ref[width - 1 - k:width - k, cols].reshape(1, 1, c)
    return a[1:].reshape(rows, c) + b_ref[:, cols]


def _ssm_inproj_kernel(tiles_per_seq, x_ref, g_ref, w_ref, cw_ref, cb_ref,
                       z_ref, xbc_ref, dt_ref, carry_ref):
    @pl.when(pl.program_id(0) % tiles_per_seq == 0)
    def _():
        carry_ref[...] = jnp.zeros_like(carry_ref)

    hn = _rms(x_ref[...], g_ref[...]).astype(BF16)
    dt_ref[...] = _dot(hn, w_ref[:, D_INNER + SSM_CONV_DIM:])
    n_chunks = SSM_CONV_DIM // N_CHUNK
    zc = D_INNER // n_chunks
    for c in range(n_chunks):
        cols = slice(c * N_CHUNK, (c + 1) * N_CHUNK)
        h = _dot(hn, w_ref[:, D_INNER + c * N_CHUNK:D_INNER + (c + 1) * N_CHUNK])
        zcols = slice(c * zc, (c + 1) * zc)
        z_ref[:, zcols] = _dot(hn, w_ref[:, zcols]).astype(BF16)
        y = _causal_conv(h, carry_ref, cw_ref, cb_ref, cols, SSM_CONV, horner=True)
        xbc_ref[:, cols] = _silu(y).astype(BF16)


def _ssm_inproj(x, g, w_in, conv_w, conv_b, layer, ssm_layer, seq):
    t = x.shape[0]
    tm = min(INPROJ_ROW_TILE, seq)
    row = lambda i: (i, 0)
    return pl.pallas_call(
        functools.partial(_ssm_inproj_kernel, seq // tm),
        grid=(t // tm,),
        in_specs=[
            pl.BlockSpec((tm, D_MODEL), row),
            _layer(layer, (1, D_MODEL)),
            _layer(ssm_layer, (D_MODEL, SSM_IN_PAD)),
            _layer(ssm_layer, (SSM_CONV, SSM_CONV_DIM)),
            _layer(ssm_layer, (1, SSM_CONV_DIM)),
        ],
        out_specs=[
            pl.BlockSpec((tm, D_INNER), row),
            pl.BlockSpec((tm, SSM_CONV_DIM), row),
            pl.BlockSpec((tm, LANES), row),
        ],
        out_shape=[
            jax.ShapeDtypeStruct((t, D_INNER), BF16),
            jax.ShapeDtypeStruct((t, SSM_CONV_DIM), BF16),
            jax.ShapeDtypeStruct((t, LANES), F32),
        ],
        scratch_shapes=[pltpu.VMEM((SUBLANES, SSM_CONV_DIM), F32)],
        compiler_params=_params(1),
        name="ssm_inproj",
    )(x, g, w_in, conv_w, conv_b)


def _ssd_chunk(rows, xbc_ref, z_ref, dt_ref, dtb_ref, alog_ref, dexp_ref, ng_ref, e_ref,
               y_ref, state_ref):
    L = SSM_CHUNK
    H = SSM_HEADS

    dt_raw = dt_ref[rows, 0:H] + dtb_ref[...]
    dt = jnp.maximum(dt_raw, 0.0) + jnp.log(1.0 + jnp.exp(-jnp.abs(dt_raw)))
    acs = dt * (-jnp.exp(alog_ref[...]) * LOG2_E)
    row_h = lax.broadcasted_iota(jnp.int32, (L, H), 0)
    k = 1
    while k < L:
        acs = acs + jnp.where(row_h >= k, pltpu.roll(acs, k, 0), 0.0)
        k *= 2
    acs_t = acs.T
    rowp_t = acs_t - jnp.log2(dt.T)
    w_t = jnp.exp2(acs_t[:, L - 1:L] - rowp_t)

    ea = jnp.exp2(acs)
    ea_hi = ea.astype(BF16)
    ea_lo = (ea - ea_hi.astype(F32)).astype(BF16)
    ea_wide = _dot(ea_hi, e_ref[...]) + _dot(ea_lo, e_ref[...])
    cd_wide = ea_wide[L - 1:L, :]

    row = lax.broadcasted_iota(jnp.int32, (L, LANES), 0)
    lane = lax.broadcasted_iota(jnp.int32, (L, LANES), 1)
    causal = row >= lane
    keep_l = jnp.where(lane < SSM_HEAD_DIM, 1.0, 0.0).astype(BF16)
    keep_r = jnp.where(lane < SSM_HEAD_DIM, 0.0, 1.0).astype(BF16)

    for g in range(SSM_GROUPS):
        b_g = xbc_ref[rows, D_INNER + g * SSM_STATE:D_INNER + (g + 1) * SSM_STATE]
        c_g = xbc_ref[rows, D_INNER + SSM_BC_DIM + g * SSM_STATE:
                      D_INNER + SSM_BC_DIM + (g + 1) * SSM_STATE]
        cb = _dot_nt(c_g, b_g)
        b_t = b_g.astype(F32).T
        gcols = slice(g * GROUP_WIDTH, (g + 1) * GROUP_WIDTH)
        y_off = _dot(c_g, state_ref[g].astype(BF16)) * ea_wide[:, gcols]
        ys = []
        for pr in range(HEADS_PER_GROUP // 2):
            heads = (g * HEADS_PER_GROUP + 2 * pr, g * HEADS_PER_GROUP + 2 * pr + 1)
            cols = slice((g * 2 + pr) * LANES, (g * 2 + pr + 1) * LANES)
            scols = slice(pr * LANES, (pr + 1) * LANES)
            xp = xbc_ref[rows, cols]
            sp = state_ref[g, :, scols]
            m_parts, bw_parts = [], []
            for h in heads:
                a_col = jnp.broadcast_to(acs[:, h:h + 1], (L, L))
                a_row = jnp.broadcast_to(rowp_t[h:h + 1, :], (L, L))
                m_parts.append(
                    (cb * jnp.exp2(jnp.where(causal, a_col - a_row, -jnp.inf))).astype(BF16))
                bw_parts.append((b_t * jnp.broadcast_to(w_t[h:h + 1, :], (L, L))).astype(BF16))
            x_blk = jnp.concatenate([xp * keep_l, xp * keep_r], axis=0)
            y_p = _dot(jnp.concatenate(m_parts, axis=1), x_blk) + y_off[:, scols]
            y_p = y_p + xp.astype(F32) * dexp_ref[:, cols]
            state_ref[g, :, scols] = sp * cd_wide[:, cols] + _dot(
                jnp.concatenate(bw_parts, axis=1), x_blk)
            ys.append(y_p)
        y_g = jnp.concatenate(ys, axis=1) * _silu(z_ref[rows, gcols].astype(F32))
        y_ref[rows, gcols] = _rms(y_g, ng_ref[:, gcols]).astype(BF16)


def _ssd_kernel(*refs):
    state_ref = refs[-1]

    @pl.when(pl.program_id(1) == 0)
    def _():
        state_ref[...] = jnp.zeros_like(state_ref)

    for sub in range(SSD_CHUNKS_PER_STEP):
        _ssd_chunk(slice(sub * SSM_CHUNK, (sub + 1) * SSM_CHUNK), *refs)


def _ssd(xbc, z, dt, dt_bias, a_log, d_wide, norm_g, expand, ssm_layer, batch, seq):
    t = xbc.shape[0]
    rows = SSD_CHUNKS_PER_STEP * SSM_CHUNK
    n_steps = seq // rows
    row = lambda b, c: (b * n_steps + c, 0)
    return pl.pallas_call(
        _ssd_kernel,
        grid=(batch, n_steps),
        in_specs=[
            pl.BlockSpec((rows, SSM_CONV_DIM), row),
            pl.BlockSpec((rows, D_INNER), row),
            pl.BlockSpec((rows, LANES), row),
            _layer(ssm_layer, (1, SSM_HEADS)),
            _layer(ssm_layer, (1, SSM_HEADS)),
            _layer(ssm_layer, (1, D_INNER)),
            _layer(ssm_layer, (1, D_INNER)),
            _resident((SSM_HEADS, D_INNER)),
        ],
        out_specs=pl.BlockSpec((rows, D_INNER), row),
        out_shape=jax.ShapeDtypeStruct((t, D_INNER), BF16),
        scratch_shapes=[pltpu.VMEM((SSM_GROUPS, SSM_STATE, GROUP_WIDTH), F32)],
        compiler_params=_params(2),
        name="ssd_scan",
    )(xbc, z, dt, dt_bias, a_log, d_wide, norm_g, expand)


def _qkv_kernel(x_ref, g_ref, w_ref, qt_ref, k_ref, vt_ref, wqt_ref, wvt_ref):
    @pl.when(pl.program_id(0) == 0)
    def _():
        wqt_ref[...] = w_ref[:, 0:D_MODEL].T
        wvt_ref[...] = w_ref[:, 2 * D_MODEL:3 * D_MODEL].T

    hn = _rms(x_ref[...], g_ref[...]).astype(BF16)
    scale = ATTN_HEAD_DIM ** -0.5 * LOG2_E
    for c in range(D_MODEL // N_CHUNK):
        rows = slice(c * N_CHUNK, (c + 1) * N_CHUNK)
        qt_ref[0, rows, :] = (_dot_nt(wqt_ref[rows, :], hn) * scale).astype(BF16)
        vt_ref[0, rows, :] = _dot_nt(wvt_ref[rows, :], hn).astype(BF16)
        k_ref[:, rows] = _dot(hn, w_ref[:, D_MODEL + c * N_CHUNK:D_MODEL + (c + 1) * N_CHUNK]
                              ).astype(BF16)


def _qkv(x, g, w_qkv, layer, attn_layer, seq):
    t = x.shape[0]
    tm = min(ATTN_TILE, seq)
    row = lambda i: (i, 0)
    tile = lambda i: (i, 0, 0)
    return pl.pallas_call(
        _qkv_kernel,
        grid=(t // tm,),
        in_specs=[
            pl.BlockSpec((tm, D_MODEL), row),
            _layer(layer, (1, D_MODEL)),
            _layer(attn_layer, (D_MODEL, 3 * D_MODEL)),
        ],
        out_specs=[
            pl.BlockSpec((1, D_MODEL, tm), tile),
            pl.BlockSpec((tm, D_MODEL), row),
            pl.BlockSpec((1, D_MODEL, tm), tile),
        ],
        out_shape=[
            jax.ShapeDtypeStruct((t // tm, D_MODEL, tm), BF16),
            jax.ShapeDtypeStruct((t, D_MODEL), BF16),
            jax.ShapeDtypeStruct((t // tm, D_MODEL, tm), BF16),
        ],
        scratch_shapes=[
            pltpu.VMEM((D_MODEL, D_MODEL), BF16),
            pltpu.VMEM((D_MODEL, D_MODEL), BF16),
        ],
        compiler_params=_params(1),
        name="attn_qkv",
    )(x, g, w_qkv)


def _attn_kernel(lambda_init, qt_ref, k_ref, vt_ref, pos_ref, slope_ref, lam_ref, g_ref,
                 o_ref, s_ref, m_ref, acc_ref):
    tq = qt_ref.shape[2]
    tk = tq
    qi = pl.program_id(2)
    heads = range(ATTN_HEADS_PER_STEP)

    r8 = lax.broadcasted_iota(jnp.int32, (SUBLANES, tq), 0)
    zero_half = jnp.zeros((ATTN_HEAD_DIM, tq), BF16)
    zero_pad = jnp.zeros((LANES - SUBLANES, tq), BF16)
    q_aug, slope_row = [], []
    for hh in heads:
        slope = jnp.concatenate([slope_ref[hh]] * (tq // LANES), axis=1) * LOG2_E
        s_hi = slope.astype(BF16).astype(F32)
        s_lo = slope - s_hi
        feat = jnp.where(r8 == 0, POS_DIGIT_BASE * s_hi,
                         jnp.where(r8 == 1, s_hi,
                                   jnp.where(r8 == 2, POS_DIGIT_BASE * s_lo,
                                             jnp.where(r8 == 3, s_lo, 0.0)))).astype(BF16)
        aug = jnp.concatenate([feat, zero_pad], axis=0)
        q_all = qt_ref[0, hh * ATTN_V_DIM:(hh + 1) * ATTN_V_DIM, :]
        q_aug.append((jnp.concatenate([q_all[0:ATTN_HEAD_DIM], zero_half, aug], axis=0),
                      jnp.concatenate([zero_half, q_all[ATTN_HEAD_DIM:], aug], axis=0)))
        slope_row.append(slope[0:1, :])
    pos = pos_ref[...]

    m_ref[...] = jnp.full(m_ref.shape, -jnp.inf, F32)
    acc_ref[...] = jnp.zeros_like(acc_ref)

    def scores(ki, slot):
        rows = pl.ds(pl.multiple_of(ki * tk, tk), tk)
        for hh in heads:
            k_aug = jnp.concatenate(
                [k_ref[rows, hh * ATTN_V_DIM:(hh + 1) * ATTN_V_DIM], pos], axis=1)
            for j in range(2):
                s_ref[slot, hh, j] = _dot(k_aug, q_aug[hh][j])

    def softmax_pv(ki, slot, mask):
        for hh in heads:
            v_blk = jnp.concatenate(
                [vt_ref[ki, hh * ATTN_V_DIM:(hh + 1) * ATTN_V_DIM, :],
                 jnp.ones((ATTN_SUM_ROWS, tk), BF16)], axis=0)
            c_blk = slope_row[hh] * ((ki - qi) * tk).astype(F32)
            for j in range(2):
                s = s_ref[slot, hh, j]
                if mask is not None:
                    s = jnp.where(mask, s, -jnp.inf)
                m_old = m_ref[hh, j]
                m_new = jnp.maximum(m_old, jnp.max(s, axis=0, keepdims=True) + c_blk)
                p = jnp.exp2(s - (m_new - c_blk))
                alpha = jnp.exp2(m_old - m_new)
                acc_ref[hh, j] = alpha * acc_ref[hh, j] + _dot(v_blk, p.astype(BF16))
                m_ref[hh, j] = m_new

    kk = lax.broadcasted_iota(jnp.int32, (tk, tq), 0)
    qq = lax.broadcasted_iota(jnp.int32, (tk, tq), 1)
    causal = kk <= qq

    @pl.when(qi == 0)
    def _():
        scores(0, 0)
        softmax_pv(0, 0, causal)

    @pl.when(qi > 0)
    def _():
        scores(0, 0)
        scores(1, 1)
        softmax_pv(0, 0, None)

        def body(i, carry):
            b0 = 2 * i + 1
            scores(b0 + 1, 0)
            softmax_pv(b0, 1, None)
            scores(b0 + 2, 1)
            softmax_pv(b0 + 1, 0, None)
            return carry

        lax.fori_loop(0, (qi - 1) // 2, body, 0)

        @pl.when(qi % 2 == 1)
        def _():
            softmax_pv(qi, 1, causal)

        @pl.when(qi % 2 == 0)
        def _():
            scores(qi, 0)
            softmax_pv(qi - 1, 1, None)
            softmax_pv(qi, 0, causal)

    lv = lam_ref[...]
    lam = (jnp.exp(jnp.sum(lv[0:1] * lv[1:2], axis=-1, keepdims=True))
           - jnp.exp(jnp.sum(lv[2:3] * lv[3:4], axis=-1, keepdims=True)) + lambda_init)
    g_col = jnp.concatenate([g_ref[...]] * (tq // LANES), axis=1)
    for hh in heads:
        a0, a1 = acc_ref[hh, 0], acc_ref[hh, 1]
        o = (a0[0:ATTN_V_DIM] / a0[ATTN_V_DIM:ATTN_V_DIM + 1]
             - lam * (a1[0:ATTN_V_DIM] / a1[ATTN_V_DIM:ATTN_V_DIM + 1]))
        o = o * lax.rsqrt(jnp.mean(o * o, axis=0, keepdims=True) + RMS_EPS) * g_col
        o_ref[:, hh * ATTN_V_DIM:(hh + 1) * ATTN_V_DIM] = (
            o * (1.0 - lambda_init)).T.astype(BF16)


def _attention(qt, k, vt, pos, slopes, lam_vecs, g_wide, lambda_init, attn_layer, batch, seq):
    t = k.shape[0]
    tq = qt.shape[2]
    nq = seq // tq
    hps = ATTN_HEADS_PER_STEP
    width = hps * ATTN_V_DIM
    return pl.pallas_call(
        functools.partial(_attn_kernel, lambda_init),
        grid=(batch, ATTN_HEADS // hps, nq),
        in_specs=[
            pl.BlockSpec((1, width, tq), lambda b, h, q: (b * nq + q, h, 0)),
            pl.BlockSpec((seq, width), lambda b, h, q: (b, h)),
            pl.BlockSpec((nq, width, tq), lambda b, h, q: (b, h, 0)),
            _resident((tq, LANES)),
            pl.BlockSpec((hps, SUBLANES, LANES), lambda b, h, q: (h, 0, 0)),
            _layer(attn_layer, (4, ATTN_HEAD_DIM)),
            _layer(attn_layer, (ATTN_V_DIM, LANES)),
        ],
        out_specs=pl.BlockSpec((tq, width), lambda b, h, q: (b * nq + q, h)),
        out_shape=jax.ShapeDtypeStruct((t, D_MODEL), BF16),
        scratch_shapes=[
            pltpu.VMEM((2, hps, 2, tq, tq), F32),
            pltpu.VMEM((hps, 2, 1, tq), F32),
            pltpu.VMEM((hps, 2, ATTN_V_DIM + ATTN_SUM_ROWS, tq), F32),
        ],
        compiler_params=_params(3),
        name="diff_attention",
    )(qt, k, vt, pos, slopes, lam_vecs, g_wide)


def _ffn_kernel(tiles_per_seq, fc, y_ref, wo_ref, gmix_ref, x_ref, gpre_ref, wup_ref, cw_ref, cb_ref,
                wdn_ref, gpost_ref, o_ref, act_ref, carry_ref):
    @pl.when(pl.program_id(0) % tiles_per_seq == 0)
    def _():
        carry_ref[...] = jnp.zeros_like(carry_ref)

    x = x_ref[...] + _rms(_dot(y_ref[...], wo_ref[...]), gmix_ref[...])
    hn = _rms(x, gpre_ref[...]).astype(BF16)
    for c in range(D_FF // fc):
        halves = []
        for part in range(2):
            cols = slice(part * D_FF + c * fc, part * D_FF + (c + 1) * fc)
            h = _dot(hn, wup_ref[:, cols])
            halves.append(_causal_conv(h, carry_ref, cw_ref, cb_ref, cols, FFN_CONV, horner=False))
        act_ref[:, c * fc:(c + 1) * fc] = (_silu(halves[0]) * halves[1]).astype(BF16)
    f = _dot(act_ref[...], wdn_ref[...])
    o_ref[...] = x + _rms(f, gpost_ref[...])


def _mixer_out_ffn(y, wo, gmix, x, gpre, wup, conv_w, conv_b, wdn, gpost, layer, mix_layer, seq):
    t, k = y.shape
    tm = min(ROW_TILE, seq)
    fc = FFN_CHUNK
    row = lambda i: (i, 0)
    return pl.pallas_call(
        functools.partial(_ffn_kernel, seq // tm, fc),
        grid=(t // tm,),
        in_specs=[
            pl.BlockSpec((tm, k), row),
            _layer(mix_layer, (k, D_MODEL)),
            _layer(layer, (1, D_MODEL)),
            pl.BlockSpec((tm, D_MODEL), row),
            _layer(layer, (1, D_MODEL)),
            _layer(layer, (D_MODEL, 2 * D_FF)),
            _layer(layer, (FFN_CONV, 2 * D_FF)),
            _layer(layer, (1, 2 * D_FF)),
            _layer(layer, (D_FF, D_MODEL)),
            _layer(layer, (1, D_MODEL)),
        ],
        out_specs=pl.BlockSpec((tm, D_MODEL), row),
        out_shape=jax.ShapeDtypeStruct((t, D_MODEL), F32),
        scratch_shapes=[
            pltpu.VMEM((tm, D_FF), BF16),
            pltpu.VMEM((SUBLANES, 2 * D_FF), F32),
        ],
        compiler_params=_params(1),
        name="mixer_out_ffn",
    )(y, wo, gmix, x, gpre, wup, conv_w, conv_b, wdn, gpost)


def _rows(v):
    return v.astype(F32)[:, None, :]


def _head_expand_matrix():
    head_of_lane = jnp.arange(D_INNER) // SSM_HEAD_DIM
    return (jnp.arange(SSM_HEADS)[:, None] == head_of_lane[None, :]).astype(BF16)


def _alibi_features(tk):
    kk = jnp.arange(tk)
    base = int(POS_DIGIT_BASE)
    cols = jnp.stack([kk // base, kk % base, kk // base, kk % base], axis=1).astype(F32)
    return jnp.pad(cols, ((0, 0), (0, LANES - 4))).astype(BF16)


def kernel(x, ssm_w_in, ssm_conv_w, ssm_conv_b, ssm_dt_bias, ssm_A_log, ssm_D, ssm_norm_g, ssm_w_out, attn_w_qkv, attn_lambda_q1, attn_lambda_k1, attn_lambda_q2, attn_lambda_k2, attn_subln_g, attn_w_out, norm_mix_pre, norm_mix_post, norm_ffn_pre, norm_ffn_post, ffn_w_up, ffn_conv_w, ffn_conv_b, ffn_w_down):
    batch, seq, d_model = x.shape
    depth = norm_mix_pre.shape[0]
    assert d_model == D_MODEL and seq % ROW_TILE == 0 and seq % ATTN_TILE == 0
    assert seq % min(INPROJ_ROW_TILE, seq) == 0 and seq % (SSD_CHUNKS_PER_STEP * SSM_CHUNK) == 0
    t = batch * seq
    xs = x.reshape(t, D_MODEL).astype(F32)

    w_in = jnp.pad(ssm_w_in, ((0, 0), (0, 0), (0, SSM_IN_PAD - SSM_IN_DIM))).astype(BF16)
    conv_w, conv_b = ssm_conv_w.astype(F32), _rows(ssm_conv_b)
    dt_bias, a_log = _rows(ssm_dt_bias), _rows(ssm_A_log)
    d_wide = _rows(jnp.repeat(ssm_D, SSM_HEAD_DIM, axis=1))
    ssm_g = _rows(ssm_norm_g)
    w_out_ssm = ssm_w_out.astype(BF16)
    w_qkv = attn_w_qkv.astype(BF16)
    lam_vecs = jnp.stack([attn_lambda_q1, attn_lambda_k1, attn_lambda_q2, attn_lambda_k2],
                         axis=1).astype(F32)
    g_wide = jnp.broadcast_to(attn_subln_g.astype(F32)[:, :, None],
                              attn_subln_g.shape + (LANES,))
    w_out_attn = attn_w_out.astype(BF16)
    g_mix_pre, g_mix_post = _rows(norm_mix_pre), _rows(norm_mix_post)
    g_ffn_pre, g_ffn_post = _rows(norm_ffn_pre), _rows(norm_ffn_post)
    w_up, w_down = ffn_w_up.astype(BF16), ffn_w_down.astype(BF16)
    f_conv_w, f_conv_b = ffn_conv_w.astype(F32), _rows(ffn_conv_b)

    expand = _head_expand_matrix()
    pos = _alibi_features(min(ATTN_TILE, seq))
    slopes = jnp.exp2(-8.0 * jnp.arange(1, ATTN_HEADS + 1, dtype=F32) / ATTN_HEADS)
    slopes = jnp.broadcast_to(slopes[:, None, None], (ATTN_HEADS, SUBLANES, LANES))

    for i in range(depth):
        j = i // 2
        if i % 2 == 0:
            z, xbc, dt = _ssm_inproj(xs, g_mix_pre, w_in, conv_w, conv_b, i, j, seq)
            mixed = _ssd(xbc, z, dt, dt_bias, a_log, d_wide, ssm_g, expand, j, batch, seq)
            w_o = w_out_ssm
        else:
            lambda_init = 0.8 - 0.6 * math.exp(-0.3 * i)
            qt, k, vt = _qkv(xs, g_mix_pre, w_qkv, i, j, seq)
            mixed = _attention(qt, k, vt, pos, slopes, lam_vecs, g_wide, lambda_init, j, batch, seq)
            w_o = w_out_attn
        xs = _mixer_out_ffn(mixed, w_o, g_mix_post, xs, g_ffn_pre, w_up, f_conv_w, f_conv_b,
                            w_down, g_ffn_post, i, j, seq)
    return xs.reshape(batch, seq, D_MODEL).astype(x.dtype)
```
